```python
import jax, jax.numpy as jnp
from jax import lax
import numpy as np

D_MODEL = 2048
BATCH = 8
SEQ = 4096
DEPTH = 1
DEC_BATCH = 16
DEC_SEQ = 16
PAST_LEN = 1024

CHUNK = 64
N_PREV_CHUNKS = 8
BAND_PAST = N_PREV_CHUNKS * CHUNK
BAND = BAND_PAST + CHUNK
C_CONV = D_MODEL // 2
C_ATTN = D_MODEL - C_CONV
HEAD_DIM = 64
N_HEADS = C_ATTN // HEAD_DIM
CONV_WIDTH = 31
REL_CLIP = 256
N_KEYS = 128
N_EXPERTS = N_KEYS * N_KEYS
PEER_HEADS = 8
PEER_TOPK = 16
D_KEY = 256
D_HALF = D_KEY // 2
PEER_BLOCK = 128
N_IN = 2 * C_CONV + 3 * C_ATTN
EPS = 1e-6

kernel_name = 'hybrid_conv_bandattn_peer_stream_step'


def rmsnorm(x, g):
    x32 = x.astype(jnp.float32)
    y = x32 * lax.rsqrt(jnp.mean(x32 * x32, axis=-1, keepdims=True) + EPS) * g.astype(jnp.float32)
    return y.astype(x.dtype)


def layernorm(x, g, b):
    x32 = x.astype(jnp.float32)
    mu = jnp.mean(x32, axis=-1, keepdims=True)
    xc = x32 - mu
    y = xc * lax.rsqrt(jnp.mean(xc * xc, axis=-1, keepdims=True) + EPS)
    return (y * g.astype(jnp.float32) + b.astype(jnp.float32)).astype(x.dtype)


def causal_dwconv(u_ctx, w, b):
    y = lax.conv_general_dilated(u_ctx, w[:, None, :], window_strides=(1,), padding='VALID',
                                 dimension_numbers=('NWC', 'WIO', 'NWC'),
                                 feature_group_count=w.shape[-1])
    return y + b


def rel_bias(table, rel):
    idx = jnp.clip(rel, -REL_CLIP, REL_CLIP) + REL_CLIP
    return table[:, idx].astype(jnp.float32)


def attend(q, k, v, bias, mask):
    s = jnp.einsum('bqhd,bkhd->bhqk', q.astype(jnp.float32), k.astype(jnp.float32)) * (HEAD_DIM ** -0.5) + bias
    if mask is not None:
        s = jnp.where(mask, s, -1e30)
    p = jax.nn.softmax(s, axis=-1)
    return jnp.einsum('bhqk,bkhd->bqhd', p.astype(v.dtype), v)


def band_attention_prompt(q, k, v, table):
    B, T, H, Dh = q.shape
    nc = T // CHUNK
    qc = q.reshape(B, nc, CHUNK, H, Dh).transpose(1, 0, 2, 3, 4)
    pad = jnp.zeros((B, BAND_PAST, H, Dh), k.dtype)
    kp = jnp.concatenate([pad, k], axis=1)
    vp = jnp.concatenate([pad, v], axis=1)
    a = jnp.arange(CHUNK)[:, None]
    bidx = jnp.arange(BAND)[None, :]
    bias = rel_bias(table, bidx - BAND_PAST - a)

    def one_chunk(args):
        c, qb = args
        kb = lax.dynamic_slice_in_dim(kp, c * CHUNK, BAND, axis=1)
        vb = lax.dynamic_slice_in_dim(vp, c * CHUNK, BAND, axis=1)
        mask = bidx >= (N_PREV_CHUNKS - c) * CHUNK
        return attend(qb, kb, vb, bias, mask)

    o = lax.map(one_chunk, (jnp.arange(nc), qc))
    return o.transpose(1, 0, 2, 3, 4).reshape(B, T, H, Dh)


def band_attention_sample(q, k, v, k_cache, v_cache, table):
    L = k_cache.shape[1]
    S = q.shape[1]
    kk = jnp.concatenate([k_cache, k], axis=1)
    vv = jnp.concatenate([v_cache, v], axis=1)
    qpos = PAST_LEN + jnp.arange(S)
    kpos = jnp.concatenate([PAST_LEN - L + jnp.arange(L), PAST_LEN + jnp.arange(S)])
    bias = rel_bias(table, kpos[None, :] - qpos[:, None])
    return attend(q, kk, vv, bias, None)


def peer(h, w_pq, sub_keys, u_tab, v_tab):
    n, d = h.shape
    n_pad = (-n) % PEER_BLOCK
    hp = jnp.pad(h, ((0, n_pad), (0, 0)))

    def one_block(hb):
        t = hb.shape[0]
        q = (hb @ w_pq).reshape(t, PEER_HEADS, 2, D_HALF).astype(jnp.float32)
        s = jnp.einsum('thpd,pnd->thpn', q, sub_keys.astype(jnp.float32))
        sv, si = lax.top_k(s, PEER_TOPK)
        cand = (sv[:, :, 0, :, None] + sv[:, :, 1, None, :]).reshape(t, PEER_HEADS, PEER_TOPK * PEER_TOPK)
        cidx = (si[:, :, 0, :, None] * N_KEYS + si[:, :, 1, None, :]).reshape(t, PEER_HEADS, PEER_TOPK * PEER_TOPK)
        tv, tpos = lax.top_k(cand, PEER_TOPK)
        eidx = jnp.take_along_axis(cidx, tpos, axis=-1).reshape(t, PEER_HEADS * PEER_TOPK)
        g = jax.nn.softmax(tv, axis=-1).reshape(t, PEER_HEADS * PEER_TOPK)
        u = jnp.take(u_tab, eidx, axis=0)
        vv = jnp.take(v_tab, eidx, axis=0)
        act = jax.nn.gelu(jnp.einsum('td,ted->te', hb, u).astype(jnp.float32))
        return jnp.einsum('te,ted->td', (g * act).astype(hb.dtype), vv)

    out = lax.map(one_block, hp.reshape(-1, PEER_BLOCK, d))
    return out.reshape(-1, d)[:n]


def modulation(c, w_ada, b_ada):
    m = jax.nn.silu(c) @ w_ada + b_ada
    return jnp.split(m[:, None, :], 6, axis=-1)


def layer(x, c, conv_left, attn_fn, norm1_g, norm2_g, w_ada, b_ada, w_in, w_dw, b_dw,
          conv_ln_g, conv_ln_b, w_out, w_pq, sub_keys, u_tab, v_tab):
    B, T, D = x.shape
    sh1, sc1, g1, sh2, sc2, g2 = modulation(c, w_ada, b_ada)
    h = rmsnorm(x, norm1_g) * (1 + sc1) + sh1
    z = h @ w_in
    za, zg, q, k, v = jnp.split(z, [C_CONV, 2 * C_CONV, 2 * C_CONV + C_ATTN, 2 * C_CONV + 2 * C_ATTN], axis=-1)
    u = za * jax.nn.sigmoid(zg)
    y_conv = causal_dwconv(jnp.concatenate([conv_left, u], axis=1), w_dw, b_dw)
    y_conv = jax.nn.silu(layernorm(y_conv, conv_ln_g, conv_ln_b))
    q = q.reshape(B, T, N_HEADS, HEAD_DIM)
    k = k.reshape(B, T, N_HEADS, HEAD_DIM)
    v = v.reshape(B, T, N_HEADS, HEAD_DIM)
    o = attn_fn(q, k, v).reshape(B, T, C_ATTN)
    x = x + g1 * (jnp.concatenate([y_conv, o], axis=-1) @ w_out)
    h2 = rmsnorm(x, norm2_g) * (1 + sc2) + sh2
    x = x + g2 * peer(h2.reshape(B * T, D), w_pq, sub_keys, u_tab, v_tab).reshape(B, T, D)
    return x, u, k, v


def setup_inputs(seed: int = 0) -> dict:
    key = jax.random.key(seed)
    ks = jax.random.split(key, 24)
    f32 = jnp.float32

    def nrm(k, shape, s):
        return s * jax.random.normal(k, shape, f32)

    cache_len = min(BAND_PAST, PAST_LEN)
    return {
        'x_prompt': nrm(ks[0], (BATCH, SEQ, D_MODEL), 1.0),
        'x_sample': nrm(ks[1], (DEC_BATCH, DEC_SEQ, D_MODEL), 1.0),
        'c_prompt': nrm(ks[2], (BATCH, D_MODEL), 1.0),
        'c_sample': nrm(ks[3], (DEC_BATCH, D_MODEL), 1.0),
        'cache_conv': nrm(ks[4], (DEPTH, DEC_BATCH, CONV_WIDTH - 1, C_CONV), 0.5),
        'cache_attn_k': nrm(ks[5], (DEPTH, DEC_BATCH, cache_len, N_HEADS, HEAD_DIM), 1.0),
        'cache_attn_v': nrm(ks[6], (DEPTH, DEC_BATCH, cache_len, N_HEADS, HEAD_DIM), 1.0),
        'norm1_g': 1.0 + nrm(ks[7], (DEPTH, D_MODEL), 0.05),
        'norm2_g': 1.0 + nrm(ks[8], (DEPTH, D_MODEL), 0.05),
        'w_ada': nrm(ks[9], (DEPTH, D_MODEL, 6 * D_MODEL), 0.5 * D_MODEL ** -0.5),
        'b_ada': nrm(ks[10], (DEPTH, 6 * D_MODEL), 0.02),
        'w_in': nrm(ks[11], (DEPTH, D_MODEL, N_IN), D_MODEL ** -0.5),
        'w_dw': nrm(ks[12], (DEPTH, CONV_WIDTH, C_CONV), CONV_WIDTH ** -0.5),
        'b_dw': nrm(ks[13], (DEPTH, C_CONV), 0.02),
        'conv_ln_g': 1.0 + nrm(ks[14], (DEPTH, C_CONV), 0.05),
        'conv_ln_b': nrm(ks[15], (DEPTH, C_CONV), 0.02),
        'rel_bias_table': nrm(ks[16], (DEPTH, N_HEADS, 2 * REL_CLIP + 1), 0.3),
        'w_out': nrm(ks[17], (DEPTH, D_MODEL, D_MODEL), D_MODEL ** -0.5),
        'w_pq': nrm(ks[18], (DEPTH, D_MODEL, PEER_HEADS * D_KEY), D_MODEL ** -0.5),
        'sub_keys': nrm(ks[19], (DEPTH, 2, N_KEYS, D_HALF), D_HALF ** -0.5),
        'u_experts': nrm(ks[20], (DEPTH, N_EXPERTS, D_MODEL), D_MODEL ** -0.5),
        'v_experts': nrm(ks[21], (DEPTH, N_EXPERTS, D_MODEL), 0.5),
        'final_g': 1.0 + nrm(ks[22], (D_MODEL,), 0.05),
    }


def reference(x_prompt, x_sample, c_prompt, c_sample, cache_conv, cache_attn_k, cache_attn_v,
              norm1_g, norm2_g, w_ada, b_ada, w_in, w_dw, b_dw, conv_ln_g, conv_ln_b,
              rel_bias_table, w_out, w_pq, sub_keys, u_experts, v_experts, final_g):
    xp, xs = x_prompt, x_sample
    conv_p, k_p, v_p, conv_s, k_s, v_s = [], [], [], [], [], []
    for l in range(DEPTH):
        params = (norm1_g[l], norm2_g[l], w_ada[l], b_ada[l], w_in[l], w_dw[l], b_dw[l],
                  conv_ln_g[l], conv_ln_b[l], w_out[l], w_pq[l], sub_keys[l], u_experts[l], v_experts[l])
        table = rel_bias_table[l]
        left0 = jnp.zeros((xp.shape[0], CONV_WIDTH - 1, C_CONV), xp.dtype)
        xp, up, kp, vp = layer(xp, c_prompt, left0,
                               lambda q, k, v: band_attention_prompt(q, k, v, table), *params)
        conv_p.append(up[:, -(CONV_WIDTH - 1):])
        k_p.append(kp[:, -BAND_PAST:])
        v_p.append(vp[:, -BAND_PAST:])
        kc, vc, cc = cache_attn_k[l], cache_attn_v[l], cache_conv[l]
        xs, us, ksn, vsn = layer(xs, c_sample, cc,
                                 lambda q, k, v: band_attention_sample(q, k, v, kc, vc, table), *params)
        conv_s.append(jnp.concatenate([cc, us], axis=1)[:, -(CONV_WIDTH - 1):])
        k_s.append(ksn)
        v_s.append(vsn)
    y_prompt = rmsnorm(xp, final_g)
    y_sample = rmsnorm(xs, final_g)
    return (y_prompt, y_sample, jnp.stack(conv_p), jnp.stack(k_p), jnp.stack(v_p),
            jnp.stack(conv_s), jnp.stack(k_s), jnp.stack(v_s))
```

```python
import functools
import math

import jax
import jax.numpy as jnp
from jax import lax
from jax.experimental import pallas as pl
from jax.experimental.pallas import tpu as pltpu

F32 = jnp.float32
BF16 = jnp.bfloat16

D_MODEL = 2048
C_CONV = 1024
C_ATTN = 1024
HEAD_DIM = 64
N_HEADS = 16
CHUNK = 64
N_PREV_CHUNKS = 8
BAND_PAST = N_PREV_CHUNKS * CHUNK
CONV_WIDTH = 31
REL_CLIP = 256
N_KEYS = 128
PEER_HEADS = 8
PEER_TOPK = 16
D_HALF = 128
EPS = 1e-6
NEG = -1e30

LANES = 128
CONV_HALO = 32
BIAS_W = 1024
VMEM_LIMIT = 56 * 1024 * 1024


def _cparams(sem):
    return pltpu.CompilerParams(dimension_semantics=sem, vmem_limit_bytes=VMEM_LIMIT)


def _mod_kernel(c_ref, w_ref, b_ref, o_ref):
    c = c_ref[...]
    sc = (c * jax.nn.sigmoid(c)).astype(BF16)
    o_ref[...] = jnp.dot(sc, w_ref[...].astype(BF16), preferred_element_type=F32) + b_ref[...]


def _modulation(c, w_ada, b_ada):
    r = c.shape[0]
    n = w_ada.shape[1]
    tn = 1536
    return pl.pallas_call(
        _mod_kernel,
        grid=(n // tn,),
        in_specs=[pl.BlockSpec((r, D_MODEL), lambda j: (0, 0)),
                  pl.BlockSpec((D_MODEL, tn), lambda j: (0, j)),
                  pl.BlockSpec((1, tn), lambda j: (0, j))],
        out_specs=pl.BlockSpec((r, tn), lambda j: (0, j)),
        out_shape=jax.ShapeDtypeStruct((r, n), F32),
        compiler_params=_cparams(("arbitrary",)),
        name="modulation",
    )(c, w_ada, b_ada.reshape(1, n))


def _in_kernel(x_ref, g_ref, sc_ref, sh_ref, wa_ref, wg_ref, wq_ref, wk_ref, wv_ref,
               u_ref, q_ref, k_ref, v_ref, h_scr):
    @pl.when(pl.program_id(1) == 0)
    def _():
        x = x_ref[...]
        h = x * lax.rsqrt(jnp.mean(x * x, axis=-1, keepdims=True) + EPS) * g_ref[...]
        h = h * (1.0 + sc_ref[0]) + sh_ref[0]
        h_scr[...] = h.astype(BF16)

    h = h_scr[...]
    za = jnp.dot(h, wa_ref[...], preferred_element_type=F32)
    zg = jnp.dot(h, wg_ref[...], preferred_element_type=F32)
    u_ref[...] = za * jax.nn.sigmoid(zg)
    q_ref[...] = jnp.dot(h, wq_ref[...], preferred_element_type=F32).astype(BF16)
    k_ref[...] = jnp.dot(h, wk_ref[...], preferred_element_type=F32)
    v_ref[...] = jnp.dot(h, wv_ref[...], preferred_element_type=F32)


def _in_proj(x, norm_g, sc, sh, w_in_bf, tm):
    n = x.shape[0]
    groups, r, _ = sc.shape
    tiles_per_group = n // tm // groups
    tn = 256
    nj = C_CONV // tn
    mod_spec = pl.BlockSpec((1, r, D_MODEL), lambda i, j: (i // tiles_per_group, 0, 0))

    def wspec(g):
        return pl.BlockSpec((D_MODEL, tn), lambda i, j: (0, g * nj + j))

    ospec = pl.BlockSpec((tm, tn), lambda i, j: (i, j))
    return pl.pallas_call(
        _in_kernel,
        grid=(n // tm, nj),
        in_specs=[pl.BlockSpec((tm, D_MODEL), lambda i, j: (i, 0)),
                  pl.BlockSpec((1, D_MODEL), lambda i, j: (0, 0)),
                  mod_spec, mod_spec, wspec(0), wspec(1), wspec(2), wspec(3), wspec(4)],
        out_specs=[ospec, ospec, ospec, ospec],
        out_shape=[jax.ShapeDtypeStruct((n, C_CONV), F32), jax.ShapeDtypeStruct((n, C_ATTN), BF16),
                   jax.ShapeDtypeStruct((n, C_ATTN), F32), jax.ShapeDtypeStruct((n, C_ATTN), F32)],
        scratch_shapes=[pltpu.VMEM((tm, D_MODEL), BF16)],
        compiler_params=_cparams(("parallel", "arbitrary")),
        name="in_proj",
    )(x, norm_g.reshape(1, D_MODEL), sc, sh, w_in_bf, w_in_bf, w_in_bf, w_in_bf, w_in_bf)


def _conv_kernel(left_ref, prev_ref, u_ref, w_ref, b_ref, lg_ref, lb_ref, o_ref, ctx, *, tc):
    i = pl.program_id(1)

    @pl.when(i == 0)
    def _():
        ctx[0:CONV_HALO, :] = left_ref[0]

    @pl.when(i > 0)
    def _():
        ctx[0:CONV_HALO, :] = prev_ref[0]

    ctx[CONV_HALO:CONV_HALO + tc, :] = u_ref[0]
    off = CONV_HALO - (CONV_WIDTH - 1)
    acc = jnp.zeros((tc, C_CONV), F32)
    for k in range(CONV_WIDTH):
        acc = acc + w_ref[k:k + 1, :] * ctx[off + k:off + k + tc, :]
    y = acc + b_ref[...]
    mu = jnp.mean(y, axis=-1, keepdims=True)
    yc = y - mu
    yn = yc * lax.rsqrt(jnp.mean(yc * yc, axis=-1, keepdims=True) + EPS)
    yn = yn * lg_ref[...] + lb_ref[...]
    o_ref[0] = (yn * jax.nn.sigmoid(yn)).astype(BF16)


def _conv_module(u, left, w_dw, b_dw, ln_g, ln_b, tc):
    b, t, c = u.shape
    nt = t // tc
    halo_per_tile = tc // CONV_HALO
    prev = u if nt > 1 else left
    return pl.pallas_call(
        functools.partial(_conv_kernel, tc=tc),
        grid=(b, nt),
        in_specs=[pl.BlockSpec((1, CONV_HALO, c), lambda bi, i: (bi, 0, 0)),
                  pl.BlockSpec((1, CONV_HALO, c), lambda bi, i: (bi, jnp.maximum(i * halo_per_tile - 1, 0), 0)),
                  pl.BlockSpec((1, tc, c), lambda bi, i: (bi, i, 0)),
                  pl.BlockSpec((CONV_WIDTH, c), lambda bi, i: (0, 0)),
                  pl.BlockSpec((1, c), lambda bi, i: (0, 0)),
                  pl.BlockSpec((1, c), lambda bi, i: (0, 0)),
                  pl.BlockSpec((1, c), lambda bi, i: (0, 0))],
        out_specs=pl.BlockSpec((1, tc, c), lambda bi, i: (bi, i, 0)),
        out_shape=jax.ShapeDtypeStruct((b, t, c), BF16),
        scratch_shapes=[pltpu.VMEM((CONV_HALO + tc, c), F32)],
        compiler_params=_cparams(("parallel", "arbitrary")),
        name="conv_module",
    )(left, prev, u, w_dw, b_dw.reshape(1, c), ln_g.reshape(1, c), ln_b.reshape(1, c))


def _bias_kernel(t_ref, o_ref, *, tq, band_mask):
    x = jnp.broadcast_to(t_ref[0], (tq, BIAS_W))
    y = pltpu.roll(x, 0, 1, stride=1, stride_axis=0)
    if band_mask:
        qc = lax.broadcasted_iota(jnp.int32, (tq, BIAS_W), 0) // CHUNK
        kc = lax.broadcasted_iota(jnp.int32, (tq, BIAS_W), 1) // CHUNK
        y = jnp.where((kc >= qc) & (kc <= qc + N_PREV_CHUNKS), y, NEG)
    o_ref[0] = y


def _expand_bias(table, tq, band_mask):
    d = jnp.arange(BIAS_W)
    d = jnp.where(d >= BIAS_W - tq, d - BIAS_W, d)
    idx = jnp.clip(d - BAND_PAST, -REL_CLIP, REL_CLIP) + REL_CLIP
    t_ext = table[:, idx].reshape(N_HEADS, 1, BIAS_W)
    return pl.pallas_call(
        functools.partial(_bias_kernel, tq=tq, band_mask=band_mask),
        grid=(N_HEADS,),
        in_specs=[pl.BlockSpec((1, 1, BIAS_W), lambda h: (h, 0, 0))],
        out_specs=pl.BlockSpec((1, tq, BIAS_W), lambda h: (h, 0, 0)),
        out_shape=jax.ShapeDtypeStruct((N_HEADS, tq, BIAS_W), F32),
        compiler_params=_cparams(("parallel",)),
        name="expand_bias",
    )(t_ext)


def _attn_kernel(q_ref, k_ref, v_ref, b_ref, o_ref, *, tq, tkw, pad_rows):
    i = pl.program_id(2)
    start = pl.multiple_of(i * tq, 8)
    q = q_ref[0]
    kw = k_ref[0, pl.ds(start, tkw), :]
    vw = v_ref[0, pl.ds(start, tkw), :]
    if pad_rows:
        valid = lax.broadcasted_iota(jnp.int32, (tq, tkw), 1) >= pad_rows - i * tq
    outs = []
    for hh in range(LANES // HEAD_DIM):
        sl = slice(hh * HEAD_DIM, (hh + 1) * HEAD_DIM)
        s = lax.dot_general(q[:, sl], kw[:, sl], (((1,), (1,)), ((), ())), preferred_element_type=F32)
        s = s * (HEAD_DIM ** -0.5) + b_ref[hh][:, :tkw]
        if pad_rows:
            s = jnp.where(valid, s, NEG)
        m = jnp.max(s, axis=-1, keepdims=True)
        p = jnp.exp(s - m)
        l = jnp.sum(p, axis=-1, keepdims=True)
        o = jnp.dot(p.astype(BF16), vw[:, sl], preferred_element_type=F32)
        outs.append(o / l)
    o_ref[0] = jnp.concatenate(outs, axis=-1).astype(BF16)


def _band_attention(q, k, v, bias, tq, tkw, pad_rows):
    b, t, c = q.shape
    tk = k.shape[1]
    hp = LANES // HEAD_DIM
    return pl.pallas_call(
        functools.partial(_attn_kernel, tq=tq, tkw=tkw, pad_rows=pad_rows),
        grid=(b, c // LANES, t // tq),
        in_specs=[pl.BlockSpec((1, tq, LANES), lambda bi, h, i: (bi, i, h)),
                  pl.BlockSpec((1, tk, LANES), lambda bi, h, i: (bi, 0, h)),
                  pl.BlockSpec((1, tk, LANES), lambda bi, h, i: (bi, 0, h)),
                  pl.BlockSpec((hp, tq, BIAS_W), lambda bi, h, i: (h, 0, 0))],
        out_specs=pl.BlockSpec((1, tq, LANES), lambda bi, h, i: (bi, i, h)),
        out_shape=jax.ShapeDtypeStruct((b, t, c), BF16),
        compiler_params=_cparams(("parallel", "parallel", "arbitrary")),
        name="band_attention",
    )(q, k, v, bias)


def _mid_kernel(yc_ref, o_ref, x_ref, w1_ref, w2_ref, g1_ref, sc_ref, sh_ref, n2_ref, x1_ref, h2t_ref):
    a = jnp.dot(yc_ref[...], w1_ref[...], preferred_element_type=F32)
    a = a + jnp.dot(o_ref[...], w2_ref[...], preferred_element_type=F32)
    x1 = x_ref[...] + g1_ref[0] * a
    x1_ref[...] = x1
    h2 = x1 * lax.rsqrt(jnp.mean(x1 * x1, axis=-1, keepdims=True) + EPS) * n2_ref[...]
    h2 = h2 * (1.0 + sc_ref[0]) + sh_ref[0]
    h2t_ref[...] = h2.T.astype(BF16)


def _mid(yc, o, x, w_out_bf, g1, sc2, sh2, norm2_g, tm):
    n = x.shape[0]
    groups, r, _ = g1.shape
    tiles_per_group = n // tm // groups
    mod_spec = pl.BlockSpec((1, r, D_MODEL), lambda i: (i // tiles_per_group, 0, 0))
    return pl.pallas_call(
        _mid_kernel,
        grid=(n // tm,),
        in_specs=[pl.BlockSpec((tm, C_CONV), lambda i: (i, 0)),
                  pl.BlockSpec((tm, C_ATTN), lambda i: (i, 0)),
                  pl.BlockSpec((tm, D_MODEL), lambda i: (i, 0)),
                  pl.BlockSpec((C_CONV, D_MODEL), lambda i: (0, 0)),
                  pl.BlockSpec((C_ATTN, D_MODEL), lambda i: (1, 0)),
                  mod_spec, mod_spec, mod_spec,
                  pl.BlockSpec((1, D_MODEL), lambda i: (0, 0))],
        out_specs=[pl.BlockSpec((tm, D_MODEL), lambda i: (i, 0)),
                   pl.BlockSpec((D_MODEL, tm), lambda i: (0, i))],
        out_shape=[jax.ShapeDtypeStruct((n, D_MODEL), F32), jax.ShapeDtypeStruct((D_MODEL, n), BF16)],
        compiler_params=_cparams(("parallel",)),
        name="out_proj_norm2",
    )(yc, o, x, w_out_bf, w_out_bf, g1, sc2, sh2, norm2_g.reshape(1, D_MODEL))


def _top16(s, pos):
    rank = jnp.full(s.shape, float(PEER_TOPK), F32)
    vals = []
    big = jnp.int32(1 << 20)
    for r in range(PEER_TOPK):
        m = jnp.max(s, axis=0, keepdims=True)
        first = jnp.min(jnp.where(s == m, pos, big), axis=0, keepdims=True)
        sel = pos == first
        rank = jnp.where(sel, float(r), rank)
        s = jnp.where(sel, -jnp.inf, s)
        vals.append(m)
    return jnp.concatenate(vals, axis=0), rank


def _prep_kernel(h_ref, w_ref, sk_ref, r2_ref, p2_ref, cnt_ref, p1_ref, q_scr, *, tt):
    q_scr[...] = jnp.dot(w_ref[...], h_ref[...], preferred_element_type=F32)
    pos_k = lax.broadcasted_iota(jnp.int32, (N_KEYS, tt), 0)
    pos_c = lax.broadcasted_iota(jnp.int32, (PEER_TOPK * PEER_TOPK, tt), 0)

    def head(h, carry):
        base = pl.multiple_of(h * 2 * D_HALF, 2 * D_HALF)
        q1 = q_scr[pl.ds(base, D_HALF), :].astype(BF16)
        q2 = q_scr[pl.ds(base + D_HALF, D_HALF), :].astype(BF16)
        s1 = jnp.dot(sk_ref[0], q1, preferred_element_type=F32)
        s2 = jnp.dot(sk_ref[1], q2, preferred_element_type=F32)
        sv1, rank1 = _top16(s1, pos_k)
        sv2, rank2 = _top16(s2, pos_k)
        cand = jnp.concatenate([sv1[a:a + 1] + sv2 for a in range(PEER_TOPK)], axis=0)
        _, crank = _top16(cand, pos_c)
        sel = jnp.where(crank < float(PEER_TOPK), 1.0, 0.0)
        e1 = jnp.exp(sv1 - sv1[0:1])
        e2 = jnp.exp(sv2 - sv2[0:1])
        cnt_rows = []
        z = jnp.zeros((1, tt), F32)
        for a in range(PEER_TOPK):
            sa = sel[a * PEER_TOPK:(a + 1) * PEER_TOPK]
            cnt_rows.append(jnp.sum(sa, axis=0, keepdims=True))
            z = z + e1[a:a + 1] * jnp.sum(sa * e2, axis=0, keepdims=True)
        cnt = jnp.zeros((N_KEYS, tt), F32)
        for a in range(PEER_TOPK):
            cnt = jnp.where(rank1 == float(a), cnt_rows[a], cnt)
        r2_ref[h] = rank2.astype(BF16)
        cnt_ref[h] = cnt
        p1_ref[h] = jnp.exp(s1 - sv1[0:1])
        p2_ref[h] = (jnp.exp(s2 - sv2[0:1]) / z).astype(BF16)
        return carry

    lax.fori_loop(0, PEER_HEADS, head, 0)


def _peer_prep(h2t, w_pq_t_bf, sub_keys_bf, tt):
    n = h2t.shape[1]
    ospec = pl.BlockSpec((PEER_HEADS, N_KEYS, tt), lambda i: (0, 0, i))
    oshape = jax.ShapeDtypeStruct((PEER_HEADS, N_KEYS, n), BF16)
    oshape32 = jax.ShapeDtypeStruct((PEER_HEADS, N_KEYS, n), F32)
    return pl.pallas_call(
        functools.partial(_prep_kernel, tt=tt),
        grid=(n // tt,),
        in_specs=[pl.BlockSpec((D_MODEL, tt), lambda i: (0, i)),
                  pl.BlockSpec((D_MODEL, D_MODEL), lambda i: (0, 0)),
                  pl.BlockSpec((2, N_KEYS, D_HALF), lambda i: (0, 0, 0))],
        out_specs=[ospec, ospec, ospec, ospec],
        out_shape=[oshape, oshape, oshape32, oshape32],
        scratch_shapes=[pltpu.VMEM((D_MODEL, tt), F32)],
        compiler_params=_cparams(("parallel",)),
        name="peer_prep",
    )(h2t, w_pq_t_bf, sub_keys_bf)


def _gelu_tanh(x):
    return 0.5 * x * (1.0 + jnp.tanh(math.sqrt(2.0 / math.pi) * (x + 0.044715 * (x * x * x))))


def _peer_kernel(h_ref, u_ref, vt_ref, r2_ref, p2_ref, cnt_ref, p1_ref, x1_ref, g2_ref, fg_ref,
                 o_ref, acc, w_scr, *, rows_per_tile):
    e = pl.program_id(1)

    @pl.when(e == 0)
    def _():
        acc[...] = jnp.zeros_like(acc)

    act = _gelu_tanh(jnp.dot(u_ref[...], h_ref[...], preferred_element_type=F32))
    for ii in range(rows_per_tile):
        i = e * rows_per_tile + ii
        g = None
        for h in range(PEER_HEADS):
            c = cnt_ref[h, pl.ds(i, 1), :].astype(BF16)
            p1 = p1_ref[h, pl.ds(i, 1), :].astype(BF16)
            t = jnp.where(r2_ref[h] < c, p2_ref[h], jnp.zeros((), BF16)) * p1
            g = t if g is None else g + t
        sl = slice(ii * N_KEYS, (ii + 1) * N_KEYS)
        w_scr[sl, :] = (act[sl, :] * g.astype(F32)).astype(BF16)
    acc[...] += jnp.dot(vt_ref[...], w_scr[...], preferred_element_type=F32)

    @pl.when(e == pl.num_programs(1) - 1)
    def _():
        x2 = x1_ref[...] + g2_ref[0] * acc[...].T
        y = x2 * lax.rsqrt(jnp.mean(x2 * x2, axis=-1, keepdims=True) + EPS) * fg_ref[...]
        o_ref[...] = y


def _peer_mix(h2t, u_bf, vt_bf, r2, p2, cnt, p1, x1, g2, final_g, tt, te):
    n = x1.shape[0]
    n_exp = u_bf.shape[0]
    groups, r, _ = g2.shape
    tiles_per_group = n // tt // groups
    pspec = pl.BlockSpec((PEER_HEADS, N_KEYS, tt), lambda i, e: (0, 0, i))
    return pl.pallas_call(
        functools.partial(_peer_kernel, rows_per_tile=te // N_KEYS),
        grid=(n // tt, n_exp // te),
        in_specs=[pl.BlockSpec((D_MODEL, tt), lambda i, e: (0, i)),
                  pl.BlockSpec((te, D_MODEL), lambda i, e: (e, 0)),
                  pl.BlockSpec((D_MODEL, te), lambda i, e: (0, e)),
                  pspec, pspec, pspec, pspec,
                  pl.BlockSpec((tt, D_MODEL), lambda i, e: (i, 0)),
                  pl.BlockSpec((1, r, D_MODEL), lambda i, e: (i // tiles_per_group, 0, 0)),
                  pl.BlockSpec((1, D_MODEL), lambda i, e: (0, 0))],
        out_specs=pl.BlockSpec((tt, D_MODEL), lambda i, e: (i, 0)),
        out_shape=jax.ShapeDtypeStruct((n, D_MODEL), F32),
        scratch_shapes=[pltpu.VMEM((D_MODEL, tt), F32), pltpu.VMEM((te, tt), BF16)],
        compiler_params=_cparams(("parallel", "arbitrary")),
        name="peer_mix",
    )(h2t, u_bf, vt_bf, r2, p2, cnt, p1, x1, g2, final_g.reshape(1, D_MODEL))


def _group(x, mods, left, k_cache, v_cache, bias, w, *, tm, tc, tq, tt, te, per_row_mod):
    b, t, _ = x.shape
    n = b * t
    if per_row_mod:
        mods = [jnp.repeat(m, t, axis=0).reshape(1, n, D_MODEL) for m in mods]
    else:
        mods = [m.reshape(b, 1, D_MODEL) for m in mods]
    sh1, sc1, g1, sh2, sc2, g2 = mods
    xf = x.reshape(n, D_MODEL)
    u, q, k, v = _in_proj(xf, w["norm1_g"], sc1, sh1, w["w_in"], tm)
    u = u.reshape(b, t, C_CONV)
    yc = _conv_module(u, left, w["w_dw"], w["b_dw"], w["ln_g"], w["ln_b"], tc)
    k3 = k.reshape(b, t, C_ATTN)
    v3 = v.reshape(b, t, C_ATTN)
    if k_cache is None:
        kk = jnp.pad(k3.astype(BF16), ((0, 0), (BAND_PAST, 0), (0, 0)))
        vv = jnp.pad(v3.astype(BF16), ((0, 0), (BAND_PAST, 0), (0, 0)))
        tkw, pad_rows = BAND_PAST + tq, BAND_PAST
    else:
        kk = jnp.concatenate([k_cache.astype(BF16), k3.astype(BF16)], axis=1)
        vv = jnp.concatenate([v_cache.astype(BF16), v3.astype(BF16)], axis=1)
        tkw, pad_rows = kk.shape[1], 0
    o = _band_attention(q.reshape(b, t, C_ATTN), kk, vv, bias, tq, tkw, pad_rows)
    x1, h2t = _mid(yc.reshape(n, C_CONV), o.reshape(n, C_ATTN), xf, w["w_out"], g1, sc2, sh2, w["norm2_g"], tm)
    r2, p2, cnt, p1 = _peer_prep(h2t, w["w_pq_t"], w["sub_keys"], tt)
    y = _peer_mix(h2t, w["u"], w["vt"], r2, p2, cnt, p1, x1, g2, w["final_g"], tt, te)
    return y.reshape(b, t, D_MODEL), u, k3, v3


def kernel(x_prompt, x_sample, c_prompt, c_sample, cache_conv, cache_attn_k, cache_attn_v, norm1_g, norm2_g,
           w_ada, b_ada, w_in, w_dw, b_dw, conv_ln_g, conv_ln_b, rel_bias_table, w_out, w_pq, sub_keys,
           u_experts, v_experts, final_g):
    depth = w_in.shape[0]
    bp, tp, _ = x_prompt.shape
    bs, ts, _ = x_sample.shape
    halo_pad = CONV_HALO - (CONV_WIDTH - 1)
    xp, xs = x_prompt, x_sample
    conv_p, k_p, v_p, conv_s, k_s, v_s = [], [], [], [], [], []
    c_all = jnp.concatenate([c_prompt, c_sample], axis=0)
    for l in range(depth):
        w = dict(norm1_g=norm1_g[l], norm2_g=norm2_g[l], w_in=w_in[l].astype(BF16), w_dw=w_dw[l], b_dw=b_dw[l],
                 ln_g=conv_ln_g[l], ln_b=conv_ln_b[l], w_out=w_out[l].astype(BF16),
                 w_pq_t=w_pq[l].T.astype(BF16), sub_keys=sub_keys[l].astype(BF16),
                 u=u_experts[l].astype(BF16), vt=v_experts[l].T.astype(BF16), final_g=final_g)
        m = _modulation(c_all, w_ada[l], b_ada[l])
        mods = [m[:, j * D_MODEL:(j + 1) * D_MODEL] for j in range(6)]
        mods_p = [mm[:bp] for mm in mods]
        mods_s = [mm[bp:] for mm in mods]
        tq_p = 4 * CHUNK
        bias_p = _expand_bias(rel_bias_table[l], tq_p, True)
        bias_s = _expand_bias(rel_bias_table[l], ts, False)

        left0 = jnp.zeros((bp, CONV_HALO, C_CONV), F32)
        xp, up, kp, vp = _group(xp, mods_p, left0, None, None, bias_p, w,
                                tm=512, tc=256, tq=tq_p, tt=512, te=512, per_row_mod=False)
        conv_p.append(up[:, -(CONV_WIDTH - 1):])
        k_p.append(kp[:, -BAND_PAST:].reshape(bp, -1, N_HEADS, HEAD_DIM))
        v_p.append(vp[:, -BAND_PAST:].reshape(bp, -1, N_HEADS, HEAD_DIM))

        cc = cache_conv[l]
        left_s = jnp.pad(cc, ((0, 0), (halo_pad, 0), (0, 0)))
        kc = cache_attn_k[l].reshape(bs, -1, C_ATTN)
        vc = cache_attn_v[l].reshape(bs, -1, C_ATTN)
        ns = bs * ts
        xs, us, ksn, vsn = _group(xs, mods_s, left_s, kc, vc, bias_s, w,
                                  tm=ns, tc=ts, tq=ts, tt=ns, te=512, per_row_mod=True)
        conv_s.append(jnp.concatenate([cc, us], axis=1)[:, -(CONV_WIDTH - 1):])
        k_s.append(ksn.reshape(bs, ts, N_HEADS, HEAD_DIM))
        v_s.append(vsn.reshape(bs, ts, N_HEADS, HEAD_DIM))
    assert depth == 1
    return (xp, xs, jnp.stack(conv_p), jnp.stack(k_p), jnp.stack(v_p),
            jnp.stack(conv_s), jnp.stack(k_s), jnp.stack(v_s))
```

```python
import functools
import math

import jax
import jax.numpy as jnp
from jax import lax
from jax.experimental import pallas as pl
from jax.experimental.pallas import tpu as pltpu

F32 = jnp.float32
BF16 = jnp.bfloat16

D_MODEL = 2048
C_CONV = 1024
C_ATTN = 1024
HEAD_DIM = 64
N_HEADS = 16
CHUNK = 64
N_PREV_CHUNKS = 8
BAND_PAST = N_PREV_CHUNKS * CHUNK
CONV_WIDTH = 31
REL_CLIP = 256
N_KEYS = 128
PEER_HEADS = 8
PEER_TOPK = 16
D_HALF = 128
EPS = 1e-6
NEG = -1e30

LANES = 128
TOK_TILE = 256
CONV_HALO = 32
BIAS_W = 1024
VMEM_LIMIT = 56 * 1024 * 1024


def _cparams(sem):
    return pltpu.CompilerParams(dimension_semantics=sem, vmem_limit_bytes=VMEM_LIMIT)


def _mod_kernel(c_ref, w_ref, b_ref, o_ref):
    c = c_ref[...]
    sc = (c * jax.nn.sigmoid(c)).astype(BF16)
    o_ref[...] = jnp.dot(sc, w_ref[...].astype(BF16), preferred_element_type=F32) + b_ref[...]


def _modulation(c, w_ada, b_ada):
    r = c.shape[0]
    n = w_ada.shape[1]
    tn = 1536
    return pl.pallas_call(
        _mod_kernel,
        grid=(n // tn,),
        in_specs=[pl.BlockSpec((r, D_MODEL), lambda j: (0, 0)),
                  pl.BlockSpec((D_MODEL, tn), lambda j: (0, j)),
                  pl.BlockSpec((1, tn), lambda j: (0, j))],
        out_specs=pl.BlockSpec((r, tn), lambda j: (0, j)),
        out_shape=jax.ShapeDtypeStruct((r, n), F32),
        compiler_params=_cparams(("arbitrary",)),
        name="modulation",
    )(c, w_ada, b_ada.reshape(1, n))


def _in_kernel(x_ref, g_ref, sc_ref, sh_ref, wa_ref, wg_ref, wq_ref, wk_ref, wv_ref,
               u_ref, q_ref, k_ref, v_ref, h_scr):
    @pl.when(pl.program_id(1) == 0)
    def _():
        x = x_ref[...]
        h = x * lax.rsqrt(jnp.mean(x * x, axis=-1, keepdims=True) + EPS) * g_ref[...]
        h = h * (1.0 + sc_ref[0]) + sh_ref[0]
        h_scr[...] = h.astype(BF16)

    h = h_scr[...]
    za = jnp.dot(h, wa_ref[...], preferred_element_type=F32)
    zg = jnp.dot(h, wg_ref[...], preferred_element_type=F32)
    u_ref[...] = za * jax.nn.sigmoid(zg)
    q_ref[...] = jnp.dot(h, wq_ref[...], preferred_element_type=F32).astype(BF16)
    k_ref[...] = jnp.dot(h, wk_ref[...], preferred_element_type=F32)
    v_ref[...] = jnp.dot(h, wv_ref[...], preferred_element_type=F32)


def _in_proj(x, norm_g, sc, sh, w_in_bf, tm):
    n = x.shape[0]
    groups, r, _ = sc.shape
    tiles_per_group = n // tm // groups
    tn = 256
    nj = C_CONV // tn
    mod_spec = pl.BlockSpec((1, r, D_MODEL), lambda i, j: (i // tiles_per_group, 0, 0))

    def wspec(g):
        return pl.BlockSpec((D_MODEL, tn), lambda i, j: (0, g * nj + j))

    ospec = pl.BlockSpec((tm, tn), lambda i, j: (i, j))
    return pl.pallas_call(
        _in_kernel,
        grid=(n // tm, nj),
        in_specs=[pl.BlockSpec((tm, D_MODEL), lambda i, j: (i, 0)),
                  pl.BlockSpec((1, D_MODEL), lambda i, j: (0, 0)),
                  mod_spec, mod_spec, wspec(0), wspec(1), wspec(2), wspec(3), wspec(4)],
        out_specs=[ospec, ospec, ospec, ospec],
        out_shape=[jax.ShapeDtypeStruct((n, C_CONV), F32), jax.ShapeDtypeStruct((n, C_ATTN), BF16),
                   jax.ShapeDtypeStruct((n, C_ATTN), F32), jax.ShapeDtypeStruct((n, C_ATTN), F32)],
        scratch_shapes=[pltpu.VMEM((tm, D_MODEL), BF16)],
        compiler_params=_cparams(("parallel", "arbitrary")),
        name="in_proj",
    )(x, norm_g.reshape(1, D_MODEL), sc, sh, w_in_bf, w_in_bf, w_in_bf, w_in_bf, w_in_bf)


def _conv_kernel(left_ref, prev_ref, u_ref, w_ref, b_ref, lg_ref, lb_ref, o_ref, ctx, *, tc):
    i = pl.program_id(1)

    @pl.when(i == 0)
    def _():
        ctx[0:CONV_HALO, :] = left_ref[0]

    @pl.when(i > 0)
    def _():
        ctx[0:CONV_HALO, :] = prev_ref[0]

    ctx[CONV_HALO:CONV_HALO + tc, :] = u_ref[0]
    off = CONV_HALO - (CONV_WIDTH - 1)
    acc = jnp.zeros((tc, C_CONV), F32)
    for k in range(CONV_WIDTH):
        acc = acc + w_ref[k:k + 1, :] * ctx[off + k:off + k + tc, :]
    y = acc + b_ref[...]
    mu = jnp.mean(y, axis=-1, keepdims=True)
    yc = y - mu
    yn = yc * lax.rsqrt(jnp.mean(yc * yc, axis=-1, keepdims=True) + EPS)
    yn = yn * lg_ref[...] + lb_ref[...]
    o_ref[0] = (yn * jax.nn.sigmoid(yn)).astype(BF16)


def _conv_module(u, left, w_dw, b_dw, ln_g, ln_b, tc):
    b, t, c = u.shape
    nt = t // tc
    halo_per_tile = tc // CONV_HALO
    prev = u if nt > 1 else left
    return pl.pallas_call(
        functools.partial(_conv_kernel, tc=tc),
        grid=(b, nt),
        in_specs=[pl.BlockSpec((1, CONV_HALO, c), lambda bi, i: (bi, 0, 0)),
                  pl.BlockSpec((1, CONV_HALO, c), lambda bi, i: (bi, jnp.maximum(i * halo_per_tile - 1, 0), 0)),
                  pl.BlockSpec((1, tc, c), lambda bi, i: (bi, i, 0)),
                  pl.BlockSpec((CONV_WIDTH, c), lambda bi, i: (0, 0)),
                  pl.BlockSpec((1, c), lambda bi, i: (0, 0)),
                  pl.BlockSpec((1, c), lambda bi, i: (0, 0)),
                  pl.BlockSpec((1, c), lambda bi, i: (0, 0))],
        out_specs=pl.BlockSpec((1, tc, c), lambda bi, i: (bi, i, 0)),
        out_shape=jax.ShapeDtypeStruct((b, t, c), BF16),
        scratch_shapes=[pltpu.VMEM((CONV_HALO + tc, c), F32)],
        compiler_params=_cparams(("parallel", "arbitrary")),
        name="conv_module",
    )(left, prev, u, w_dw, b_dw.reshape(1, c), ln_g.reshape(1, c), ln_b.reshape(1, c))


def _bias_kernel(t_ref, o_ref, *, tq, band_mask):
    x = jnp.broadcast_to(t_ref[0], (tq, BIAS_W))
    y = pltpu.roll(x, 0, 1, stride=1, stride_axis=0)
    if band_mask:
        qc = lax.broadcasted_iota(jnp.int32, (tq, BIAS_W), 0) // CHUNK
        kc = lax.broadcasted_iota(jnp.int32, (tq, BIAS_W), 1) // CHUNK
        y = jnp.where((kc >= qc) & (kc <= qc + N_PREV_CHUNKS), y, NEG)
    o_ref[0] = y


def _expand_bias(table, tq, band_mask):
    d = jnp.arange(BIAS_W)
    d = jnp.where(d >= BIAS_W - tq, d - BIAS_W, d)
    idx = jnp.clip(d - BAND_PAST, -REL_CLIP, REL_CLIP) + REL_CLIP
    t_ext = table[:, idx].reshape(N_HEADS, 1, BIAS_W)
    return pl.pallas_call(
        functools.partial(_bias_kernel, tq=tq, band_mask=band_mask),
        grid=(N_HEADS,),
        in_specs=[pl.BlockSpec((1, 1, BIAS_W), lambda h: (h, 0, 0))],
        out_specs=pl.BlockSpec((1, tq, BIAS_W), lambda h: (h, 0, 0)),
        out_shape=jax.ShapeDtypeStruct((N_HEADS, tq, BIAS_W), F32),
        compiler_params=_cparams(("parallel",)),
        name="expand_bias",
    )(t_ext)


def _attn_kernel(q_ref, k_ref, v_ref, b_ref, o_ref, *, tq, tkw, pad_rows):
    i = pl.program_id(2)
    start = pl.multiple_of(i * tq, 8)
    q = q_ref[0]
    kw = k_ref[0, pl.ds(start, tkw), :]
    vw = v_ref[0, pl.ds(start, tkw), :]
    if pad_rows:
        valid = lax.broadcasted_iota(jnp.int32, (tq, tkw), 1) >= pad_rows - i * tq
    outs = []
    for hh in range(LANES // HEAD_DIM):
        sl = slice(hh * HEAD_DIM, (hh + 1) * HEAD_DIM)
        s = lax.dot_general(q[:, sl], kw[:, sl], (((1,), (1,)), ((), ())), preferred_element_type=F32)
        s = s * (HEAD_DIM ** -0.5) + b_ref[hh][:, :tkw]
        if pad_rows:
            s = jnp.where(valid, s, NEG)
        m = jnp.max(s, axis=-1, keepdims=True)
        p = jnp.exp(s - m)
        l = jnp.sum(p, axis=-1, keepdims=True)
        o = jnp.dot(p.astype(BF16), vw[:, sl], preferred_element_type=F32)
        outs.append(o / l)
    o_ref[0] = jnp.concatenate(outs, axis=-1).astype(BF16)


def _band_attention(q, k, v, bias, tq, tkw, pad_rows):
    b, t, c = q.shape
    tk = k.shape[1]
    hp = LANES // HEAD_DIM
    return pl.pallas_call(
        functools.partial(_attn_kernel, tq=tq, tkw=tkw, pad_rows=pad_rows),
        grid=(b, c // LANES, t // tq),
        in_specs=[pl.BlockSpec((1, tq, LANES), lambda bi, h, i: (bi, i, h)),
                  pl.BlockSpec((1, tk, LANES), lambda bi, h, i: (bi, 0, h)),
                  pl.BlockSpec((1, tk, LANES), lambda bi, h, i: (bi, 0, h)),
                  pl.BlockSpec((hp, tq, BIAS_W), lambda bi, h, i: (h, 0, 0))],
        out_specs=pl.BlockSpec((1, tq, LANES), lambda bi, h, i: (bi, i, h)),
        out_shape=jax.ShapeDtypeStruct((b, t, c), BF16),
        compiler_params=_cparams(("parallel", "parallel", "arbitrary")),
        name="band_attention",
    )(q, k, v, bias)


def _mid_kernel(yc_ref, o_ref, x_ref, w1_ref, w2_ref, g1_ref, sc_ref, sh_ref, n2_ref, x1_ref, h2t_ref):
    a = jnp.dot(yc_ref[...], w1_ref[...], preferred_element_type=F32)
    a = a + jnp.dot(o_ref[...], w2_ref[...], preferred_element_type=F32)
    x1 = x_ref[...] + g1_ref[0] * a
    x1_ref[...] = x1
    h2 = x1 * lax.rsqrt(jnp.mean(x1 * x1, axis=-1, keepdims=True) + EPS) * n2_ref[...]
    h2 = h2 * (1.0 + sc_ref[0]) + sh_ref[0]
    h2t_ref[...] = h2.T.astype(BF16)


def _mid(yc, o, x, w_out_bf, g1, sc2, sh2, norm2_g, tm):
    n = x.shape[0]
    groups, r, _ = g1.shape
    tiles_per_group = n // tm // groups
    mod_spec = pl.BlockSpec((1, r, D_MODEL), lambda i: (i // tiles_per_group, 0, 0))
    return pl.pallas_call(
        _mid_kernel,
        grid=(n // tm,),
        in_specs=[pl.BlockSpec((tm, C_CONV), lambda i: (i, 0)),
                  pl.BlockSpec((tm, C_ATTN), lambda i: (i, 0)),
                  pl.BlockSpec((tm, D_MODEL), lambda i: (i, 0)),
                  pl.BlockSpec((C_CONV, D_MODEL), lambda i: (0, 0)),
                  pl.BlockSpec((C_ATTN, D_MODEL), lambda i: (1, 0)),
                  mod_spec, mod_spec, mod_spec,
                  pl.BlockSpec((1, D_MODEL), lambda i: (0, 0))],
        out_specs=[pl.BlockSpec((tm, D_MODEL), lambda i: (i, 0)),
                   pl.BlockSpec((D_MODEL, tm), lambda i: (0, i))],
        out_shape=[jax.ShapeDtypeStruct((n, D_MODEL), F32), jax.ShapeDtypeStruct((D_MODEL, n), BF16)],
        compiler_params=_cparams(("parallel",)),
        name="out_proj_norm2",
    )(yc, o, x, w_out_bf, w_out_bf, g1, sc2, sh2, norm2_g.reshape(1, D_MODEL))


def _top16(s, pos):
    rank = jnp.full(s.shape, float(PEER_TOPK), F32)
    vals = []
    big = jnp.int32(1 << 20)
    for r in range(PEER_TOPK):
        m = jnp.max(s, axis=0, keepdims=True)
        first = jnp.min(jnp.where(s == m, pos, big), axis=0, keepdims=True)
        sel = pos == first
        rank = jnp.where(sel, float(r), rank)
        s = jnp.where(sel, -jnp.inf, s)
        vals.append(m)
    return jnp.concatenate(vals, axis=0), rank


def _select_exact(s1, s2):
    t = s1.shape[1]
    pos_k = lax.broadcasted_iota(jnp.int32, (N_KEYS, t), 0)
    pos_c = lax.broadcasted_iota(jnp.int32, (PEER_TOPK * PEER_TOPK, t), 0)
    sv1, rank1 = _top16(s1, pos_k)
    sv2, rank2 = _top16(s2, pos_k)
    cand = jnp.concatenate([sv1[a:a + 1] + sv2 for a in range(PEER_TOPK)], axis=0)
    _, crank = _top16(cand, pos_c)
    sel = jnp.where(crank < float(PEER_TOPK), 1.0, 0.0)
    e1 = jnp.exp(sv1 - sv1[0:1])
    e2 = jnp.exp(sv2 - sv2[0:1])
    cnt_rows = []
    z = jnp.zeros((1, t), F32)
    for a in range(PEER_TOPK):
        sa = sel[a * PEER_TOPK:(a + 1) * PEER_TOPK]
        cnt_rows.append(jnp.sum(sa, axis=0, keepdims=True))
        z = z + e1[a:a + 1] * jnp.sum(sa * e2, axis=0, keepdims=True)
    cnt = jnp.zeros((N_KEYS, t), F32)
    for a in range(PEER_TOPK):
        cnt = jnp.where(rank1 == float(a), cnt_rows[a], cnt)
    return rank2, cnt, z


INT_MIN = -(1 << 31)
TAKEN = INT_MIN + PEER_TOPK
MASKED = INT_MIN + 1024
FIRST_LIMIT = (8, 5, 4, 3, 2, 2, 2)


def _order_key(s):
    b = lax.bitcast_convert_type(s, jnp.int32)
    return b ^ ((b >> 31) & jnp.int32(0x7FFFFFFF))


def _key_value(k):
    return lax.bitcast_convert_type(k ^ ((k >> 31) & jnp.int32(0x7FFFFFFF)), F32)


def _mark_top16(k, tops_ref=None):
    for r in range(PEER_TOPK):
        m = jnp.max(k, axis=0, keepdims=True)
        k = jnp.where(k == m, jnp.int32(INT_MIN + r), k)
        if tops_ref is not None:
            tops_ref[r:r + 1, :] = m
    return k


def _select_fast(s1, s2, tops1, tops2):
    t = s1.shape[1]
    k1 = _mark_top16(_order_key(s1), tops1)
    k2 = _mark_top16(_order_key(s2), tops2)
    sv1 = _key_value(tops1[...])
    sv2 = _key_value(tops2[...])
    lo, hi = sv1[0:8], sv1[8:16]
    row8 = lax.broadcasted_iota(jnp.int32, (8, t), 0)
    slabs = [_order_key(lo + sv2[0:1]), _order_key(hi + sv2[0:1])]
    for b, lim in enumerate(FIRST_LIMIT, start=1):
        kb = _order_key(lo + sv2[b:b + 1])
        slabs.append(kb if lim == 8 else jnp.where(row8 < lim, kb, jnp.int32(MASKED)))
    slabs.append(_order_key(sv1[0:1] + sv2[8:16]))
    kc = _mark_top16(jnp.concatenate(slabs, axis=0))
    mem = jnp.where(kc < TAKEN, 1.0, 0.0)
    e1 = jnp.exp(sv1 - sv1[0:1])
    e2 = jnp.exp(sv2 - sv2[0:1])
    cnt_lo = mem[0:8]
    zacc = mem[0:8] * e1[0:8] * e2[0:1] + mem[8:16] * e1[8:16] * e2[0:1]
    for b in range(1, 8):
        mb = mem[8 + 8 * b:16 + 8 * b]
        cnt_lo = cnt_lo + mb
        zacc = zacc + mb * e1[0:8] * e2[b:b + 1]
    last = mem[72:80]
    cnt_lo = cnt_lo + jnp.where(row8 == 0, jnp.sum(last, axis=0, keepdims=True), 0.0)
    zacc = zacc + last * e1[0:1] * e2[8:16]
    cnt_hi = mem[8:16]
    z = jnp.sum(zacc, axis=0, keepdims=True)
    n1 = jnp.sum(jnp.where(k1 < TAKEN, 1.0, 0.0), axis=0, keepdims=True)
    n2 = jnp.sum(jnp.where(k2 < TAKEN, 1.0, 0.0), axis=0, keepdims=True)
    crowded = jnp.maximum(jnp.maximum(n1, n2), jnp.sum(mem, axis=0, keepdims=True))
    rank2 = jnp.where(k2 < TAKEN, (k2 & 31).astype(F32), float(PEER_TOPK))
    cnt_rows = []
    for g in range(N_KEYS // 8):
        kg = k1[8 * g:8 * g + 8]
        r = kg & 7
        c = jnp.where((kg & 8) == 0, jnp.take_along_axis(cnt_lo, r, axis=0),
                      jnp.take_along_axis(cnt_hi, r, axis=0))
        cnt_rows.append(jnp.where(kg < TAKEN, c, 0.0))
    return rank2, jnp.concatenate(cnt_rows, axis=0), z, crowded


def _prep_kernel(h_ref, w_ref, sk_ref, r2_ref, p2_ref, cnt_ref, p1_ref, q_scr, s_scr, tops1, tops2, *, tt):
    q_scr[...] = jnp.dot(w_ref[...], h_ref[...], preferred_element_type=F32)

    def head(h, carry):
        base = pl.multiple_of(h * 2 * D_HALF, 2 * D_HALF)
        q1 = q_scr[pl.ds(base, D_HALF), :].astype(BF16)
        q2 = q_scr[pl.ds(base + D_HALF, D_HALF), :].astype(BF16)
        s1_all = jnp.dot(sk_ref[0], q1, preferred_element_type=F32)
        s2_all = jnp.dot(sk_ref[1], q2, preferred_element_type=F32)
        for lt in range(tt // TOK_TILE):
            s_scr[lt, 0] = s1_all[:, lt * TOK_TILE:(lt + 1) * TOK_TILE]
            s_scr[lt, 1] = s2_all[:, lt * TOK_TILE:(lt + 1) * TOK_TILE]

        def lane_tile(lt, c2):
            s1 = s_scr[lt, 0]
            s2 = s_scr[lt, 1]
            rank2, cnt, z, crowded = _select_fast(s1, s2, tops1, tops2)
            e2 = jnp.exp(s2 - jnp.max(s2, axis=0, keepdims=True))
            r2_ref[h, lt] = rank2.astype(BF16)
            cnt_ref[h, lt] = cnt
            p1_ref[h, lt] = jnp.exp(s1 - jnp.max(s1, axis=0, keepdims=True))
            p2_ref[h, lt] = (e2 / z).astype(BF16)

            @pl.when(jnp.max(crowded) > float(PEER_TOPK))
            def _():
                rank2x, cntx, zx = _select_exact(s1, s2)
                r2_ref[h, lt] = rank2x.astype(BF16)
                cnt_ref[h, lt] = cntx
                p2_ref[h, lt] = (e2 / zx).astype(BF16)

            return c2

        lax.fori_loop(0, tt // TOK_TILE, lane_tile, 0)
        return carry

    lax.fori_loop(0, PEER_HEADS, head, 0)


def _peer_prep(h2t, w_pq_t_bf, sub_keys_bf, tt):
    n = h2t.shape[1]
    ospec = pl.BlockSpec((PEER_HEADS, tt // TOK_TILE, N_KEYS, TOK_TILE), lambda i: (0, i, 0, 0))
    oshape = jax.ShapeDtypeStruct((PEER_HEADS, n // TOK_TILE, N_KEYS, TOK_TILE), BF16)
    oshape32 = jax.ShapeDtypeStruct((PEER_HEADS, n // TOK_TILE, N_KEYS, TOK_TILE), F32)
    return pl.pallas_call(
        functools.partial(_prep_kernel, tt=tt),
        grid=(n // tt,),
        in_specs=[pl.BlockSpec((D_MODEL, tt), lambda i: (0, i)),
                  pl.BlockSpec((D_MODEL, D_MODEL), lambda i: (0, 0)),
                  pl.BlockSpec((2, N_KEYS, D_HALF), lambda i: (0, 0, 0))],
        out_specs=[ospec, ospec, ospec, ospec],
        out_shape=[oshape, oshape, oshape32, oshape32],
        scratch_shapes=[pltpu.VMEM((D_MODEL, tt), F32), pltpu.VMEM((tt // TOK_TILE, 2, N_KEYS, TOK_TILE), F32),
                        pltpu.VMEM((PEER_TOPK, TOK_TILE), jnp.int32), pltpu.VMEM((PEER_TOPK, TOK_TILE), jnp.int32)],
        compiler_params=_cparams(("parallel",)),
        name="peer_prep",
    )(h2t, w_pq_t_bf, sub_keys_bf)


def _gelu_tanh(x):
    return 0.5 * x * (1.0 + jnp.tanh(math.sqrt(2.0 / math.pi) * (x + 0.044715 * (x * x * x))))


def _gate_kernel(h_ref, u_ref, r2_ref, p2_ref, cnt_ref, p1_ref, w_ref, *, rows_per_tile):
    e = pl.program_id(1)
    act = _gelu_tanh(jnp.dot(u_ref[...], h_ref[...], preferred_element_type=F32))
    for ii in range(rows_per_tile):
        i = e * rows_per_tile + ii
        rows = slice(ii * N_KEYS, (ii + 1) * N_KEYS)
        for lt in range(w_ref.shape[1] // TOK_TILE):
            g = None
            for h in range(PEER_HEADS):
                c = cnt_ref[h, lt, pl.ds(i, 1), :].astype(BF16)
                p1 = p1_ref[h, lt, pl.ds(i, 1), :].astype(BF16)
                t = jnp.where(r2_ref[h, lt] < c, p2_ref[h, lt], jnp.zeros((), BF16)) * p1
                g = t if g is None else g + t
            cols = slice(lt * TOK_TILE, (lt + 1) * TOK_TILE)
            w_ref[rows, cols] = act[rows, cols].astype(BF16) * g


def _peer_gate(h2t, u_bf, r2, p2, cnt, p1, tt, te):
    n = h2t.shape[1]
    n_exp = u_bf.shape[0]
    pspec = pl.BlockSpec((PEER_HEADS, tt // TOK_TILE, N_KEYS, TOK_TILE), lambda i, e: (0, i, 0, 0))
    return pl.pallas_call(
        functools.partial(_gate_kernel, rows_per_tile=te // N_KEYS),
        grid=(n // tt, n_exp // te),
        in_specs=[pl.BlockSpec((D_MODEL, tt), lambda i, e: (0, i)),
                  pl.BlockSpec((te, D_MODEL), lambda i, e: (e, 0)),
                  pspec, pspec, pspec, pspec],
        out_specs=pl.BlockSpec((te, tt), lambda i, e: (e, i)),
        out_shape=jax.ShapeDtypeStruct((n_exp, n), BF16),
        compiler_params=_cparams(("parallel", "arbitrary")),
        name="peer_gate",
    )(h2t, u_bf, r2, p2, cnt, p1)


def _mix_kernel(vt_ref, w_ref, x1_ref, g2_ref, fg_ref, o_ref, acc):
    e = pl.program_id(1)

    @pl.when(e == 0)
    def _():
        acc[...] = jnp.zeros_like(acc)

    acc[...] += jnp.dot(vt_ref[...], w_ref[...], preferred_element_type=F32)

    @pl.when(e == pl.num_programs(1) - 1)
    def _():
        x2 = x1_ref[...] + g2_ref[0] * acc[...].T
        y = x2 * lax.rsqrt(jnp.mean(x2 * x2, axis=-1, keepdims=True) + EPS) * fg_ref[...]
        o_ref[...] = y


def _peer_mix(vt_bf, wt, x1, g2, final_g, tt, tk):
    n = x1.shape[0]
    n_exp = vt_bf.shape[1]
    groups, r, _ = g2.shape
    tiles_per_group = n // tt // groups
    return pl.pallas_call(
        _mix_kernel,
        grid=(n // tt, n_exp // tk),
        in_specs=[pl.BlockSpec((D_MODEL, tk), lambda i, e: (0, e)),
                  pl.BlockSpec((tk, tt), lambda i, e: (e, i)),
                  pl.BlockSpec((tt, D_MODEL), lambda i, e: (i, 0)),
                  pl.BlockSpec((1, r, D_MODEL), lambda i, e: (i // tiles_per_group, 0, 0)),
                  pl.BlockSpec((1, D_MODEL), lambda i, e: (0, 0))],
        out_specs=pl.BlockSpec((tt, D_MODEL), lambda i, e: (i, 0)),
        out_shape=jax.ShapeDtypeStruct((n, D_MODEL), F32),
        scratch_shapes=[pltpu.VMEM((D_MODEL, tt), F32)],
        compiler_params=_cparams(("parallel", "arbitrary")),
        name="peer_mix",
    )(vt_bf, wt, x1, g2, final_g.reshape(1, D_MODEL))


def _group(x, mods, left, k_cache, v_cache, bias, w, *, tm, tc, tq, tt, tg, te, per_row_mod):
    b, t, _ = x.shape
    n = b * t
    if per_row_mod:
        mods = [jnp.repeat(m, t, axis=0).reshape(1, n, D_MODEL) for m in mods]
    else:
        mods = [m.reshape(b, 1, D_MODEL) for m in mods]
    sh1, sc1, g1, sh2, sc2, g2 = mods
    xf = x.reshape(n, D_MODEL)
    u, q, k, v = _in_proj(xf, w["norm1_g"], sc1, sh1, w["w_in"], tm)
    u = u.reshape(b, t, C_CONV)
    yc = _conv_module(u, left, w["w_dw"], w["b_dw"], w["ln_g"], w["ln_b"], tc)
    k3 = k.reshape(b, t, C_ATTN)
    v3 = v.reshape(b, t, C_ATTN)
    if k_cache is None:
        kk = jnp.pad(k3.astype(BF16), ((0, 0), (BAND_PAST, 0), (0, 0)))
        vv = jnp.pad(v3.astype(BF16), ((0, 0), (BAND_PAST, 0), (0, 0)))
        tkw, pad_rows = BAND_PAST + tq, BAND_PAST
    else:
        kk = jnp.concatenate([k_cache.astype(BF16), k3.astype(BF16)], axis=1)
        vv = jnp.concatenate([v_cache.astype(BF16), v3.astype(BF16)], axis=1)
        tkw, pad_rows = kk.shape[1], 0
    o = _band_attention(q.reshape(b, t, C_ATTN), kk, vv, bias, tq, tkw, pad_rows)
    x1, h2t = _mid(yc.reshape(n, C_CONV), o.reshape(n, C_ATTN), xf, w["w_out"], g1, sc2, sh2, w["norm2_g"], tm)
    r2, p2, cnt, p1 = _peer_prep(h2t, w["w_pq_t"], w["sub_keys"], tt)
    wt = _peer_gate(h2t, w["u"], r2, p2, cnt, p1, tg, te)
    y = _peer_mix(w["vt"], wt, x1, g2, w["final_g"], tm, 2 * te)
    return y.reshape(b, t, D_MODEL), u, k3, v3


def kernel(x_prompt, x_sample, c_prompt, c_sample, cache_conv, cache_attn_k, cache_attn_v, norm1_g, norm2_g,
           w_ada, b_ada, w_in, w_dw, b_dw, conv_ln_g, conv_ln_b, rel_bias_table, w_out, w_pq, sub_keys,
           u_experts, v_experts, final_g):
    depth = w_in.shape[0]
    bp, tp, _ = x_prompt.shape
    bs, ts, _ = x_sample.shape
    halo_pad = CONV_HALO - (CONV_WIDTH - 1)
    xp, xs = x_prompt, x_sample
    conv_p, k_p, v_p, conv_s, k_s, v_s = [], [], [], [], [], []
    c_all = jnp.concatenate([c_prompt, c_sample], axis=0)
    for l in range(depth):
        w = dict(norm1_g=norm1_g[l], norm2_g=norm2_g[l], w_in=w_in[l].astype(BF16), w_dw=w_dw[l], b_dw=b_dw[l],
                 ln_g=conv_ln_g[l], ln_b=conv_ln_b[l], w_out=w_out[l].astype(BF16),
                 w_pq_t=w_pq[l].T.astype(BF16), sub_keys=sub_keys[l].astype(BF16),
                 u=u_experts[l].astype(BF16), vt=v_experts[l].T.astype(BF16), final_g=final_g)
        m = _modulation(c_all, w_ada[l], b_ada[l])
        mods = [m[:, j * D_MODEL:(j + 1) * D_MODEL] for j in range(6)]
        mods_p = [mm[:bp] for mm in mods]
        mods_s = [mm[bp:] for mm in mods]
        tq_p = 4 * CHUNK
        bias_p = _expand_bias(rel_bias_table[l], tq_p, True)
        bias_s = _expand_bias(rel_bias_table[l], ts, False)

        left0 = jnp.zeros((bp, CONV_HALO, C_CONV), F32)
        xp, up, kp, vp = _group(xp, mods_p, left0, None, None, bias_p, w,
                                tm=512, tc=256, tq=tq_p, tt=512, tg=1024, te=512, per_row_mod=False)
        conv_p.append(up[:, -(CONV_WIDTH - 1):])
        k_p.append(kp[:, -BAND_PAST:].reshape(bp, -1, N_HEADS, HEAD_DIM))
        v_p.append(vp[:, -BAND_PAST:].reshape(bp, -1, N_HEADS, HEAD_DIM))

        cc = cache_conv[l]
        left_s = jnp.pad(cc, ((0, 0), (halo_pad, 0), (0, 0)))
        kc = cache_attn_k[l].reshape(bs, -1, C_ATTN)
        vc = cache_attn_v[l].reshape(bs, -1, C_ATTN)
        ns = bs * ts
        xs, us, ksn, vsn = _group(xs, mods_s, left_s, kc, vc, bias_s, w,
                                  tm=ns, tc=ts, tq=ts, tt=ns, tg=ns, te=512, per_row_mod=True)
        conv_s.append(jnp.concatenate([cc, us], axis=1)[:, -(CONV_WIDTH - 1):])
        k_s.append(ksn.reshape(bs, ts, N_HEADS, HEAD_DIM))
        v_s.append(vsn.reshape(bs, ts, N_HEADS, HEAD_DIM))
    assert depth == 1
    return (xp, xs, jnp.stack(conv_p), jnp.stack(k_p), jnp.stack(v_p),
            jnp.stack(conv_s), jnp.stack(k_s), jnp.stack(v_s))
```

```python
import functools
import math

import jax
import jax.numpy as jnp
from jax import lax
from jax.experimental import pallas as pl
from jax.experimental.pallas import tpu as pltpu

F32 = jnp.float32
BF16 = jnp.bfloat16

D_MODEL = 2048
C_CONV = 1024
C_ATTN = 1024
HEAD_DIM = 64
N_HEADS = 16
CHUNK = 64
N_PREV_CHUNKS = 8
BAND_PAST = N_PREV_CHUNKS * CHUNK
CONV_WIDTH = 31
REL_CLIP = 256
N_KEYS = 128
PEER_HEADS = 8
PEER_TOPK = 16
D_HALF = 128
EPS = 1e-6
NEG = -1e30

LANES = 128
TOK_TILE = 256
SUBLANES = 8
CONV_HALO = 32
CONV_ROW_BLOCK = 128
CONV_COL_BLOCK = 256
BIAS_W = 1024
VMEM_LIMIT = 56 * 1024 * 1024


def _cparams(sem):
    return pltpu.CompilerParams(dimension_semantics=sem, vmem_limit_bytes=VMEM_LIMIT)


def _mod_kernel(c_ref, w_ref, b_ref, o_ref):
    c = c_ref[...]
    sc = (c * jax.nn.sigmoid(c)).astype(BF16)
    o_ref[...] = jnp.dot(sc, w_ref[...].astype(BF16), preferred_element_type=F32) + b_ref[...]


def _modulation(c, w_ada, b_ada):
    r = c.shape[0]
    n = w_ada.shape[1]
    tn = 1536
    return pl.pallas_call(
        _mod_kernel,
        grid=(n // tn,),
        in_specs=[pl.BlockSpec((r, D_MODEL), lambda j: (0, 0)),
                  pl.BlockSpec((D_MODEL, tn), lambda j: (0, j)),
                  pl.BlockSpec((1, tn), lambda j: (0, j))],
        out_specs=pl.BlockSpec((r, tn), lambda j: (0, j)),
        out_shape=jax.ShapeDtypeStruct((r, n), F32),
        compiler_params=_cparams(("arbitrary",)),
        name="modulation",
    )(c, w_ada, b_ada.reshape(1, n))


def _in_kernel(x_ref, g_ref, sc_ref, sh_ref, wa_ref, wg_ref, wq_ref, wk_ref, wv_ref,
               u_ref, q_ref, k_ref, v_ref, h_scr):
    @pl.when(pl.program_id(1) == 0)
    def _():
        x = x_ref[...]
        h = x * lax.rsqrt(jnp.mean(x * x, axis=-1, keepdims=True) + EPS) * g_ref[...]
        h = h * (1.0 + sc_ref[0]) + sh_ref[0]
        h_scr[...] = h.astype(BF16)

    h = h_scr[...]
    za = jnp.dot(h, wa_ref[...], preferred_element_type=F32)
    zg = jnp.dot(h, wg_ref[...], preferred_element_type=F32)
    u_ref[...] = za * jax.nn.sigmoid(zg)
    q_ref[...] = jnp.dot(h, wq_ref[...], preferred_element_type=F32).astype(BF16)
    k_ref[...] = jnp.dot(h, wk_ref[...], preferred_element_type=F32)
    v_ref[...] = jnp.dot(h, wv_ref[...], preferred_element_type=F32)


def _in_proj(x, norm_g, sc, sh, w_in_bf, tm):
    n = x.shape[0]
    groups, r, _ = sc.shape
    tiles_per_group = n // tm // groups
    tn = 256
    nj = C_CONV // tn
    mod_spec = pl.BlockSpec((1, r, D_MODEL), lambda i, j: (i // tiles_per_group, 0, 0))

    def wspec(g):
        return pl.BlockSpec((D_MODEL, tn), lambda i, j: (0, g * nj + j))

    ospec = pl.BlockSpec((tm, tn), lambda i, j: (i, j))
    return pl.pallas_call(
        _in_kernel,
        grid=(n // tm, nj),
        in_specs=[pl.BlockSpec((tm, D_MODEL), lambda i, j: (i, 0)),
                  pl.BlockSpec((1, D_MODEL), lambda i, j: (0, 0)),
                  mod_spec, mod_spec, wspec(0), wspec(1), wspec(2), wspec(3), wspec(4)],
        out_specs=[ospec, ospec, ospec, ospec],
        out_shape=[jax.ShapeDtypeStruct((n, C_CONV), F32), jax.ShapeDtypeStruct((n, C_ATTN), BF16),
                   jax.ShapeDtypeStruct((n, C_ATTN), F32), jax.ShapeDtypeStruct((n, C_ATTN), F32)],
        scratch_shapes=[pltpu.VMEM((tm, D_MODEL), BF16)],
        compiler_params=_cparams(("parallel", "arbitrary")),
        name="in_proj",
    )(x, norm_g.reshape(1, D_MODEL), sc, sh, w_in_bf, w_in_bf, w_in_bf, w_in_bf, w_in_bf)


def _conv_kernel(left_ref, prev_ref, u_ref, w_ref, b_ref, lg_ref, lb_ref, o_ref, ctx, shifted, y_scr, *, tc):
    i = pl.program_id(1)

    @pl.when(i == 0)
    def _():
        ctx[0:CONV_HALO, :] = left_ref[0]

    @pl.when(i > 0)
    def _():
        ctx[0:CONV_HALO, :] = prev_ref[0]

    ctx[CONV_HALO:CONV_HALO + tc, :] = u_ref[0]
    off = CONV_HALO - (CONV_WIDTH - 1)
    span = tc + CONV_HALO - SUBLANES
    for r in range(1, SUBLANES):
        shifted[r - 1, 0:span, :] = ctx[r:r + span, :]
    rb = min(tc, CONV_ROW_BLOCK)
    for c0 in range(0, C_CONV, CONV_COL_BLOCK):
        cols = slice(c0, c0 + CONV_COL_BLOCK)
        for r0 in range(0, tc, rb):
            acc = jnp.zeros((rb, CONV_COL_BLOCK), F32)
            for k in range(CONV_WIDTH):
                r = (off + k) % SUBLANES
                base = off + k - r + r0
                win = ctx[base:base + rb, cols] if r == 0 else shifted[r - 1, base:base + rb, cols]
                acc = acc + w_ref[k:k + 1, cols] * win
            y_scr[r0:r0 + rb, cols] = acc
    y = y_scr[...] + b_ref[...]
    mu = jnp.mean(y, axis=-1, keepdims=True)
    yc = y - mu
    yn = yc * lax.rsqrt(jnp.mean(yc * yc, axis=-1, keepdims=True) + EPS)
    yn = yn * lg_ref[...] + lb_ref[...]
    o_ref[0] = (yn * jax.nn.sigmoid(yn)).astype(BF16)


def _conv_module(u, left, w_dw, b_dw, ln_g, ln_b, tc):
    b, t, c = u.shape
    nt = t // tc
    halo_per_tile = tc // CONV_HALO
    prev = u if nt > 1 else left
    return pl.pallas_call(
        functools.partial(_conv_kernel, tc=tc),
        grid=(b, nt),
        in_specs=[pl.BlockSpec((1, CONV_HALO, c), lambda bi, i: (bi, 0, 0)),
                  pl.BlockSpec((1, CONV_HALO, c), lambda bi, i: (bi, jnp.maximum(i * halo_per_tile - 1, 0), 0)),
                  pl.BlockSpec((1, tc, c), lambda bi, i: (bi, i, 0)),
                  pl.BlockSpec((CONV_WIDTH, c), lambda bi, i: (0, 0)),
                  pl.BlockSpec((1, c), lambda bi, i: (0, 0)),
                  pl.BlockSpec((1, c), lambda bi, i: (0, 0)),
                  pl.BlockSpec((1, c), lambda bi, i: (0, 0))],
        out_specs=pl.BlockSpec((1, tc, c), lambda bi, i: (bi, i, 0)),
        out_shape=jax.ShapeDtypeStruct((b, t, c), BF16),
        scratch_shapes=[pltpu.VMEM((CONV_HALO + tc, c), F32),
                        pltpu.VMEM((SUBLANES - 1, CONV_HALO + tc, c), F32),
                        pltpu.VMEM((tc, c), F32)],
        compiler_params=_cparams(("parallel", "arbitrary")),
        name="conv_module",
    )(left, prev, u, w_dw, b_dw.reshape(1, c), ln_g.reshape(1, c), ln_b.reshape(1, c))


def _bias_kernel(t_ref, o_ref, *, tq, band_mask):
    x = jnp.broadcast_to(t_ref[0], (tq, BIAS_W))
    y = pltpu.roll(x, 0, 1, stride=1, stride_axis=0)
    if band_mask:
        qc = lax.broadcasted_iota(jnp.int32, (tq, BIAS_W), 0) // CHUNK
        kc = lax.broadcasted_iota(jnp.int32, (tq, BIAS_W), 1) // CHUNK
        y = jnp.where((kc >= qc) & (kc <= qc + N_PREV_CHUNKS), y, NEG)
    o_ref[0] = y


def _expand_bias(table, tq, band_mask):
    d = jnp.arange(BIAS_W)
    d = jnp.where(d >= BIAS_W - tq, d - BIAS_W, d)
    idx = jnp.clip(d - BAND_PAST, -REL_CLIP, REL_CLIP) + REL_CLIP
    t_ext = table[:, idx].reshape(N_HEADS, 1, BIAS_W)
    return pl.pallas_call(
        functools.partial(_bias_kernel, tq=tq, band_mask=band_mask),
        grid=(N_HEADS,),
        in_specs=[pl.BlockSpec((1, 1, BIAS_W), lambda h: (h, 0, 0))],
        out_specs=pl.BlockSpec((1, tq, BIAS_W), lambda h: (h, 0, 0)),
        out_shape=jax.ShapeDtypeStruct((N_HEADS, tq, BIAS_W), F32),
        compiler_params=_cparams(("parallel",)),
        name="expand_bias",
    )(t_ext)


def _attn_kernel(q_ref, k_ref, v_ref, b_ref, o_ref, *, tq, tkw, pad_rows):
    i = pl.program_id(2)
    start = pl.multiple_of(i * tq, 8)
    q = q_ref[0]
    kw = k_ref[0, pl.ds(start, tkw), :]
    vw = v_ref[0, pl.ds(start, tkw), :]
    if pad_rows:
        valid = lax.broadcasted_iota(jnp.int32, (tq, tkw), 1) >= pad_rows - i * tq
    outs = []
    for hh in range(LANES // HEAD_DIM):
        sl = slice(hh * HEAD_DIM, (hh + 1) * HEAD_DIM)
        s = lax.dot_general(q[:, sl], kw[:, sl], (((1,), (1,)), ((), ())), preferred_element_type=F32)
        s = s * (HEAD_DIM ** -0.5) + b_ref[hh][:, :tkw]
        if pad_rows:
            s = jnp.where(valid, s, NEG)
        m = jnp.max(s, axis=-1, keepdims=True)
        p = jnp.exp(s - m)
        l = jnp.sum(p, axis=-1, keepdims=True)
        o = jnp.dot(p.astype(BF16), vw[:, sl], preferred_element_type=F32)
        outs.append(o / l)
    o_ref[0] = jnp.concatenate(outs, axis=-1).astype(BF16)


def _band_attention(q, k, v, bias, tq, tkw, pad_rows):
    b, t, c = q.shape
    tk = k.shape[1]
    hp = LANES // HEAD_DIM
    return pl.pallas_call(
        functools.partial(_attn_kernel, tq=tq, tkw=tkw, pad_rows=pad_rows),
        grid=(b, c // LANES, t // tq),
        in_specs=[pl.BlockSpec((1, tq, LANES), lambda bi, h, i: (bi, i, h)),
                  pl.BlockSpec((1, tk, LANES), lambda bi, h, i: (bi, 0, h)),
                  pl.BlockSpec((1, tk, LANES), lambda bi, h, i: (bi, 0, h)),
                  pl.BlockSpec((hp, tq, BIAS_W), lambda bi, h, i: (h, 0, 0))],
        out_specs=pl.BlockSpec((1, tq, LANES), lambda bi, h, i: (bi, i, h)),
        out_shape=jax.ShapeDtypeStruct((b, t, c), BF16),
        compiler_params=_cparams(("parallel", "parallel", "arbitrary")),
        name="band_attention",
    )(q, k, v, bias)


def _mid_kernel(yc_ref, o_ref, x_ref, w1_ref, w2_ref, g1_ref, sc_ref, sh_ref, n2_ref, x1_ref, h2t_ref):
    a = jnp.dot(yc_ref[...], w1_ref[...], preferred_element_type=F32)
    a = a + jnp.dot(o_ref[...], w2_ref[...], preferred_element_type=F32)
    x1 = x_ref[...] + g1_ref[0] * a
    x1_ref[...] = x1
    h2 = x1 * lax.rsqrt(jnp.mean(x1 * x1, axis=-1, keepdims=True) + EPS) * n2_ref[...]
    h2 = h2 * (1.0 + sc_ref[0]) + sh_ref[0]
    h2t_ref[...] = h2.T.astype(BF16)


def _mid(yc, o, x, w_out_bf, g1, sc2, sh2, norm2_g, tm):
    n = x.shape[0]
    groups, r, _ = g1.shape
    tiles_per_group = n // tm // groups
    mod_spec = pl.BlockSpec((1, r, D_MODEL), lambda i: (i // tiles_per_group, 0, 0))
    return pl.pallas_call(
        _mid_kernel,
        grid=(n // tm,),
        in_specs=[pl.BlockSpec((tm, C_CONV), lambda i: (i, 0)),
                  pl.BlockSpec((tm, C_ATTN), lambda i: (i, 0)),
                  pl.BlockSpec((tm, D_MODEL), lambda i: (i, 0)),
                  pl.BlockSpec((C_CONV, D_MODEL), lambda i: (0, 0)),
                  pl.BlockSpec((C_ATTN, D_MODEL), lambda i: (1, 0)),
                  mod_spec, mod_spec, mod_spec,
                  pl.BlockSpec((1, D_MODEL), lambda i: (0, 0))],
        out_specs=[pl.BlockSpec((tm, D_MODEL), lambda i: (i, 0)),
                   pl.BlockSpec((D_MODEL, tm), lambda i: (0, i))],
        out_shape=[jax.ShapeDtypeStruct((n, D_MODEL), F32), jax.ShapeDtypeStruct((D_MODEL, n), BF16)],
        compiler_params=_cparams(("parallel",)),
        name="out_proj_norm2",
    )(yc, o, x, w_out_bf, w_out_bf, g1, sc2, sh2, norm2_g.reshape(1, D_MODEL))


def _top16(s, pos):
    rank = jnp.full(s.shape, float(PEER_TOPK), F32)
    vals = []
    big = jnp.int32(1 << 20)
    for r in range(PEER_TOPK):
        m = jnp.max(s, axis=0, keepdims=True)
        first = jnp.min(jnp.where(s == m, pos, big), axis=0, keepdims=True)
        sel = pos == first
        rank = jnp.where(sel, float(r), rank)
        s = jnp.where(sel, -jnp.inf, s)
        vals.append(m)
    return jnp.concatenate(vals, axis=0), rank


def _select_exact(s1, s2):
    t = s1.shape[1]
    pos_k = lax.broadcasted_iota(jnp.int32, (N_KEYS, t), 0)
    pos_c = lax.broadcasted_iota(jnp.int32, (PEER_TOPK * PEER_TOPK, t), 0)
    sv1, rank1 = _top16(s1, pos_k)
    sv2, rank2 = _top16(s2, pos_k)
    cand = jnp.concatenate([sv1[a:a + 1] + sv2 for a in range(PEER_TOPK)], axis=0)
    _, crank = _top16(cand, pos_c)
    sel = jnp.where(crank < float(PEER_TOPK), 1.0, 0.0)
    e1 = jnp.exp(sv1 - sv1[0:1])
    e2 = jnp.exp(sv2 - sv2[0:1])
    cnt_rows = []
    z = jnp.zeros((1, t), F32)
    for a in range(PEER_TOPK):
        sa = sel[a * PEER_TOPK:(a + 1) * PEER_TOPK]
        cnt_rows.append(jnp.sum(sa, axis=0, keepdims=True))
        z = z + e1[a:a + 1] * jnp.sum(sa * e2, axis=0, keepdims=True)
    cnt = jnp.zeros((N_KEYS, t), F32)
    for a in range(PEER_TOPK):
        cnt = jnp.where(rank1 == float(a), cnt_rows[a], cnt)
    return rank2, cnt, z


MARK_BASE = 2.0 ** 126
TAKEN = -(2.0 ** 125)
MARK_RANK_SHIFT = 18


def _mark_top16(arrays, tops_refs):
    arrays = list(arrays)
    for r in range(PEER_TOPK):
        for n, (s, tops_ref) in enumerate(zip(arrays, tops_refs)):
            m = jnp.max(s, axis=0, keepdims=True)
            arrays[n] = jnp.where(s == m, -MARK_BASE * (1.0 + r / 32.0), s)
            if tops_ref is not None:
                tops_ref[r:r + 1, :] = m
    return arrays


def _mark_rank(s):
    return (lax.bitcast_convert_type(s, jnp.int32) >> MARK_RANK_SHIFT) & 31


def _select_fast(s1, s2, tops1, tops2):
    t = s1.shape[1]
    k1, k2 = _mark_top16((s1, s2), (tops1, tops2))
    sv1 = tops1[...]
    sv2 = tops2[...]
    lo, hi = sv1[0:8], sv1[8:16]
    row8 = lax.broadcasted_iota(jnp.int32, (8, t), 0)
    slabs = [lo + sv2[0:1], hi + sv2[0:1]] + [lo + sv2[b:b + 1] for b in range(1, 8)] + [sv1[0:1] + sv2[8:16]]
    kc, = _mark_top16((jnp.concatenate(slabs, axis=0),), (None,))
    mem = jnp.where(kc <= TAKEN, 1.0, 0.0)
    e1 = jnp.exp(sv1 - sv1[0:1])
    e2 = jnp.exp(sv2 - sv2[0:1])
    cnt_lo = mem[0:8]
    zacc = mem[0:8] * e1[0:8] * e2[0:1] + mem[8:16] * e1[8:16] * e2[0:1]
    for b in range(1, 8):
        mb = mem[8 + 8 * b:16 + 8 * b]
        cnt_lo = cnt_lo + mb
        zacc = zacc + mb * e1[0:8] * e2[b:b + 1]
    last = mem[72:80]
    cnt_lo = cnt_lo + jnp.where(row8 == 0, jnp.sum(last, axis=0, keepdims=True), 0.0)
    zacc = zacc + last * e1[0:1] * e2[8:16]
    cnt_hi = mem[8:16]
    z = jnp.sum(zacc, axis=0, keepdims=True)
    n1 = jnp.sum(jnp.where(k1 <= TAKEN, 1.0, 0.0), axis=0, keepdims=True)
    n2 = jnp.sum(jnp.where(k2 <= TAKEN, 1.0, 0.0), axis=0, keepdims=True)
    crowded = jnp.maximum(jnp.maximum(n1, n2), jnp.sum(mem, axis=0, keepdims=True))
    rank2 = jnp.where(k2 <= TAKEN, _mark_rank(k2).astype(F32), float(PEER_TOPK))
    cnt_rows = []
    for g in range(N_KEYS // 8):
        kg = k1[8 * g:8 * g + 8]
        rg = _mark_rank(kg)
        r = rg & 7
        c = jnp.where((rg & 8) == 0, jnp.take_along_axis(cnt_lo, r, axis=0),
                      jnp.take_along_axis(cnt_hi, r, axis=0))
        cnt_rows.append(jnp.where(kg <= TAKEN, c, 0.0))
    return rank2, jnp.concatenate(cnt_rows, axis=0), z, crowded


def _prep_kernel(h_ref, w_ref, sk_ref, r2_ref, p2_ref, cnt_ref, p1_ref, q_scr, s_scr, tops1, tops2, *, tt):
    q_scr[...] = jnp.dot(w_ref[...], h_ref[...], preferred_element_type=F32)

    def head(h, carry):
        base = pl.multiple_of(h * 2 * D_HALF, 2 * D_HALF)
        q1 = q_scr[pl.ds(base, D_HALF), :].astype(BF16)
        q2 = q_scr[pl.ds(base + D_HALF, D_HALF), :].astype(BF16)
        s1_all = jnp.dot(sk_ref[0], q1, preferred_element_type=F32)
        s2_all = jnp.dot(sk_ref[1], q2, preferred_element_type=F32)
        for lt in range(tt // TOK_TILE):
            s_scr[lt, 0] = s1_all[:, lt * TOK_TILE:(lt + 1) * TOK_TILE]
            s_scr[lt, 1] = s2_all[:, lt * TOK_TILE:(lt + 1) * TOK_TILE]

        def lane_tile(lt, c2):
            s1 = s_scr[lt, 0]
            s2 = s_scr[lt, 1]
            rank2, cnt, z, crowded = _select_fast(s1, s2, tops1, tops2)
            e2 = jnp.exp(s2 - jnp.max(s2, axis=0, keepdims=True))
            r2_ref[h, lt] = rank2.astype(BF16)
            cnt_ref[h, lt] = cnt
            p1_ref[h, lt] = jnp.exp(s1 - jnp.max(s1, axis=0, keepdims=True))
            p2_ref[h, lt] = (e2 / z).astype(BF16)

            @pl.when(jnp.max(crowded) > float(PEER_TOPK))
            def _():
                rank2x, cntx, zx = _select_exact(s1, s2)
                r2_ref[h, lt] = rank2x.astype(BF16)
                cnt_ref[h, lt] = cntx
                p2_ref[h, lt] = (e2 / zx).astype(BF16)

            return c2

        lax.fori_loop(0, tt // TOK_TILE, lane_tile, 0)
        return carry

    lax.fori_loop(0, PEER_HEADS, head, 0)


def _peer_prep(h2t, w_pq_t_bf, sub_keys_bf, tt):
    n = h2t.shape[1]
    ospec = pl.BlockSpec((PEER_HEADS, tt // TOK_TILE, N_KEYS, TOK_TILE), lambda i: (0, i, 0, 0))
    oshape = jax.ShapeDtypeStruct((PEER_HEADS, n // TOK_TILE, N_KEYS, TOK_TILE), BF16)
    oshape32 = jax.ShapeDtypeStruct((PEER_HEADS, n // TOK_TILE, N_KEYS, TOK_TILE), F32)
    return pl.pallas_call(
        functools.partial(_prep_kernel, tt=tt),
        grid=(n // tt,),
        in_specs=[pl.BlockSpec((D_MODEL, tt), lambda i: (0, i)),
                  pl.BlockSpec((D_MODEL, D_MODEL), lambda i: (0, 0)),
                  pl.BlockSpec((2, N_KEYS, D_HALF), lambda i: (0, 0, 0))],
        out_specs=[ospec, ospec, ospec, ospec],
        out_shape=[oshape, oshape, oshape32, oshape32],
        scratch_shapes=[pltpu.VMEM((D_MODEL, tt), F32), pltpu.VMEM((tt // TOK_TILE, 2, N_KEYS, TOK_TILE), F32),
                        pltpu.VMEM((PEER_TOPK, TOK_TILE), F32), pltpu.VMEM((PEER_TOPK, TOK_TILE), F32)],
        compiler_params=_cparams(("parallel",)),
        name="peer_prep",
    )(h2t, w_pq_t_bf, sub_keys_bf)


GELU_C = math.sqrt(2.0 / math.pi)


def _gelu_tanh_bf16(x):
    x = x.astype(BF16)
    hx = 0.5 * x
    return hx + hx * jnp.tanh(x * (GELU_C + (GELU_C * 0.044715) * (x * x)))


def _gate_kernel(h_ref, u_ref, r2_ref, p2_ref, cnt_ref, p1_ref, w_ref, *, rows_per_tile):
    e = pl.program_id(1)
    act = _gelu_tanh_bf16(jnp.dot(u_ref[...], h_ref[...], preferred_element_type=F32))
    for ii in range(rows_per_tile):
        i = e * rows_per_tile + ii
        rows = slice(ii * N_KEYS, (ii + 1) * N_KEYS)
        for lt in range(w_ref.shape[1] // TOK_TILE):
            g = None
            for h in range(PEER_HEADS):
                c = cnt_ref[h, lt, pl.ds(i, 1), :].astype(BF16)
                p1 = p1_ref[h, lt, pl.ds(i, 1), :].astype(BF16)
                t = jnp.where(r2_ref[h, lt] < c, p2_ref[h, lt], jnp.zeros((), BF16)) * p1
                g = t if g is None else g + t
            cols = slice(lt * TOK_TILE, (lt + 1) * TOK_TILE)
            w_ref[rows, cols] = act[rows, cols] * g


def _peer_gate(h2t, u_bf, r2, p2, cnt, p1, tt, te):
    n = h2t.shape[1]
    n_exp = u_bf.shape[0]
    pspec = pl.BlockSpec((PEER_HEADS, tt // TOK_TILE, N_KEYS, TOK_TILE), lambda i, e: (0, i, 0, 0))
    return pl.pallas_call(
        functools.partial(_gate_kernel, rows_per_tile=te // N_KEYS),
        grid=(n // tt, n_exp // te),
        in_specs=[pl.BlockSpec((D_MODEL, tt), lambda i, e: (0, i)),
                  pl.BlockSpec((te, D_MODEL), lambda i, e: (e, 0)),
                  pspec, pspec, pspec, pspec],
        out_specs=pl.BlockSpec((te, tt), lambda i, e: (e, i)),
        out_shape=jax.ShapeDtypeStruct((n_exp, n), BF16),
        compiler_params=_cparams(("parallel", "arbitrary")),
        name="peer_gate",
    )(h2t, u_bf, r2, p2, cnt, p1)


def _mix_kernel(v_ref, w_ref, x1_ref, g2_ref, fg_ref, o_ref, acc):
    e = pl.program_id(1)

    @pl.when(e == 0)
    def _():
        acc[...] = jnp.zeros_like(acc)

    acc[...] += lax.dot_general(w_ref[...], v_ref[...], (((0,), (0,)), ((), ())), preferred_element_type=F32)

    @pl.when(e == pl.num_programs(1) - 1)
    def _():
        x2 = x1_ref[...] + g2_ref[0] * acc[...]
        y = x2 * lax.rsqrt(jnp.mean(x2 * x2, axis=-1, keepdims=True) + EPS) * fg_ref[...]
        o_ref[...] = y


def _peer_mix(v_bf, wt, x1, g2, final_g, tt, tk):
    n = x1.shape[0]
    n_exp = v_bf.shape[0]
    groups, r, _ = g2.shape
    tiles_per_group = n // tt // groups
    return pl.pallas_call(
        _mix_kernel,
        grid=(n // tt, n_exp // tk),
        in_specs=[pl.BlockSpec((tk, D_MODEL), lambda i, e: (e, 0)),
                  pl.BlockSpec((tk, tt), lambda i, e: (e, i)),
                  pl.BlockSpec((tt, D_MODEL), lambda i, e: (i, 0)),
                  pl.BlockSpec((1, r, D_MODEL), lambda i, e: (i // tiles_per_group, 0, 0)),
                  pl.BlockSpec((1, D_MODEL), lambda i, e: (0, 0))],
        out_specs=pl.BlockSpec((tt, D_MODEL), lambda i, e: (i, 0)),
        out_shape=jax.ShapeDtypeStruct((n, D_MODEL), F32),
        scratch_shapes=[pltpu.VMEM((tt, D_MODEL), F32)],
        compiler_params=_cparams(("parallel", "arbitrary")),
        name="peer_mix",
    )(v_bf, wt, x1, g2, final_g.reshape(1, D_MODEL))


def _group(x, mods, left, k_cache, v_cache, bias, w, *, tm, tc, tq, tt, tg, te, per_row_mod):
    b, t, _ = x.shape
    n = b * t
    if per_row_mod:
        mods = [jnp.repeat(m, t, axis=0).reshape(1, n, D_MODEL) for m in mods]
    else:
        mods = [m.reshape(b, 1, D_MODEL) for m in mods]
    sh1, sc1, g1, sh2, sc2, g2 = mods
    xf = x.reshape(n, D_MODEL)
    u, q, k, v = _in_proj(xf, w["norm1_g"], sc1, sh1, w["w_in"], tg)
    u = u.reshape(b, t, C_CONV)
    yc = _conv_module(u, left, w["w_dw"], w["b_dw"], w["ln_g"], w["ln_b"], tc)
    k3 = k.reshape(b, t, C_ATTN)
    v3 = v.reshape(b, t, C_ATTN)
    if k_cache is None:
        kk = jnp.pad(k3.astype(BF16), ((0, 0), (BAND_PAST, 0), (0, 0)))
        vv = jnp.pad(v3.astype(BF16), ((0, 0), (BAND_PAST, 0), (0, 0)))
        tkw, pad_rows = BAND_PAST + tq, BAND_PAST
    else:
        kk = jnp.concatenate([k_cache.astype(BF16), k3.astype(BF16)], axis=1)
        vv = jnp.concatenate([v_cache.astype(BF16), v3.astype(BF16)], axis=1)
        tkw, pad_rows = kk.shape[1], 0
    o = _band_attention(q.reshape(b, t, C_ATTN), kk, vv, bias, tq, tkw, pad_rows)
    x1, h2t = _mid(yc.reshape(n, C_CONV), o.reshape(n, C_ATTN), xf, w["w_out"], g1, sc2, sh2, w["norm2_g"], tm)
    r2, p2, cnt, p1 = _peer_prep(h2t, w["w_pq_t"], w["sub_keys"], tt)
    wt = _peer_gate(h2t, w["u"], r2, p2, cnt, p1, tg, te)
    y = _peer_mix(w["v"], wt, x1, g2, w["final_g"], tm, 2 * te)
    return y.reshape(b, t, D_MODEL), u, k3, v3


def kernel(x_prompt, x_sample, c_prompt, c_sample, cache_conv, cache_attn_k, cache_attn_v, norm1_g, norm2_g,
           w_ada, b_ada, w_in, w_dw, b_dw, conv_ln_g, conv_ln_b, rel_bias_table, w_out, w_pq, sub_keys,
           u_experts, v_experts, final_g):
    depth = w_in.shape[0]
    bp, tp, _ = x_prompt.shape
    bs, ts, _ = x_sample.shape
    halo_pad = CONV_HALO - (CONV_WIDTH - 1)
    xp, xs = x_prompt, x_sample
    conv_p, k_p, v_p, conv_s, k_s, v_s = [], [], [], [], [], []
    c_all = jnp.concatenate([c_prompt, c_sample], axis=0)
    for l in range(depth):
        w = dict(norm1_g=norm1_g[l], norm2_g=norm2_g[l], w_in=w_in[l].astype(BF16), w_dw=w_dw[l], b_dw=b_dw[l],
                 ln_g=conv_ln_g[l], ln_b=conv_ln_b[l], w_out=w_out[l].astype(BF16),
                 w_pq_t=w_pq[l].T.astype(BF16), sub_keys=sub_keys[l].astype(BF16),
                 u=u_experts[l].astype(BF16), v=v_experts[l].astype(BF16), final_g=final_g)
        m = _modulation(c_all, w_ada[l], b_ada[l])
        mods = [m[:, j * D_MODEL:(j + 1) * D_MODEL] for j in range(6)]
        mods_p = [mm[:bp] for mm in mods]
        mods_s = [mm[bp:] for mm in mods]
        tq_p = 4 * CHUNK
        bias_p = _expand_bias(rel_bias_table[l], tq_p, True)
        bias_s = _expand_bias(rel_bias_table[l], ts, False)

        left0 = jnp.zeros((bp, CONV_HALO, C_CONV), F32)
        xp, up, kp, vp = _group(xp, mods_p, left0, None, None, bias_p, w,
                                tm=512, tc=256, tq=tq_p, tt=512, tg=1024, te=512, per_row_mod=False)
        conv_p.append(up[:, -(CONV_WIDTH - 1):])
        k_p.append(kp[:, -BAND_PAST:].reshape(bp, -1, N_HEADS, HEAD_DIM))
        v_p.append(vp[:, -BAND_PAST:].reshape(bp, -1, N_HEADS, HEAD_DIM))

        cc = cache_conv[l]
        left_s = jnp.pad(cc, ((0, 0), (halo_pad, 0), (0, 0)))
        kc = cache_attn_k[l].reshape(bs, -1, C_ATTN)
        vc = cache_attn_v[l].reshape(bs, -1, C_ATTN)
        ns = bs * ts
        xs, us, ksn, vsn = _group(xs, mods_s, left_s, kc, vc, bias_s, w,
                                  tm=ns, tc=ts, tq=ts, tt=ns, tg=ns, te=512, per_row_mod=True)
        conv_s.append(jnp.concatenate([cc, us], axis=1)[:, -(CONV_WIDTH - 1):])
        k_s.append(ksn.reshape(bs, ts, N_HEADS, HEAD_DIM))
        v_s.append(vsn.reshape(bs, ts, N_HEADS, HEAD_DIM))
    assert depth == 1
    return (xp, xs, jnp.stack(conv_p), jnp.stack(k_p), jnp.stack(v_p),
            jnp.stack(conv_s), jnp.stack(k_s), jnp.stack(v_s))
```
